```python
import jax, jax.numpy as jnp
from jax import lax
import numpy as np

D_MODEL = 2048
BATCH = 1
SEQ = 16384
DEPTH = 2

SB_HEAD_DIM = 128
SB_HEADS = D_MODEL // (2 * SB_HEAD_DIM)
SB_WIDTH = SB_HEADS * SB_HEAD_DIM
Q_BLOCK = 128
K_CHUNK = 128
SSM_WIDTH = D_MODEL // 2
SSM_GROUP = 16
SSM_GROUPS = SSM_WIDTH // SSM_GROUP
SSM_STATE = 64
DT_MIN = 0.001
DT_MAX = 0.1
IN_COLS = 3 * SB_WIDTH + SSM_WIDTH + 2 * D_MODEL
D_FF_DENSE = 2 * D_MODEL
N_EXPERTS = 8
TOP_K = 2
D_FF_EXPERT = D_MODEL // 2
N_DENSE = (DEPTH + 1) // 2
N_MOE = DEPTH // 2
RMS_EPS = 1e-6

kernel_name = 'hybrid_stickbreak_s5_moe_trunk'


def rms_norm(x, g):
    x32 = x.astype(jnp.float32)
    y = x32 * lax.rsqrt(jnp.mean(x32 * x32, axis=-1, keepdims=True) + RMS_EPS)
    return (y * g.astype(jnp.float32)).astype(x.dtype)


def suffix_sum(m, n_chunks):
    mc = m.reshape(m.shape[:-1] + (n_chunks, K_CHUNK))
    ci = jnp.arange(K_CHUNK)
    tri = (ci[:, None] >= ci[None, :]).astype(m.dtype)
    ni = jnp.arange(n_chunks)
    strict = (ni[:, None] > ni[None, :]).astype(m.dtype)
    within = jnp.einsum('bhqnc,cd->bhqnd', mc, tri)
    after = jnp.einsum('bhqn,np->bhqp', jnp.sum(mc, axis=-1), strict)
    return (within + after[..., None]).reshape(m.shape)


def stick_breaking_attention(q, k, v):
    b, s, h, dh = q.shape
    nb = s // Q_BLOCK
    scale = dh ** -0.5
    qh = q.transpose(0, 2, 1, 3)
    kh = k.transpose(0, 2, 1, 3)
    vh = v.transpose(0, 2, 1, 3)
    outs = []
    for blk in range(nb):
        lk = (blk + 1) * Q_BLOCK
        q_blk = qh[:, :, blk * Q_BLOCK:lk]
        z = jnp.einsum('bhqd,bhkd->bhqk', q_blk, kh[:, :, :lk],
                       preferred_element_type=jnp.float32) * scale
        q_pos = blk * Q_BLOCK + jnp.arange(Q_BLOCK)
        causal = jnp.arange(lk)[None, :] < q_pos[:, None]
        sp = jnp.where(causal, jax.nn.softplus(z), 0.0)
        r = suffix_sum(sp, lk // K_CHUNK)
        w = jnp.exp(jnp.where(causal, z - r, -jnp.inf))
        outs.append(jnp.einsum('bhqk,bhkd->bhqd', w.astype(vh.dtype), vh[:, :, :lk]))
    out = jnp.concatenate(outs, axis=2)
    return out.transpose(0, 2, 1, 3).reshape(b, s, h * dh)


def s5_ssm(u, a_re, a_im, log_dt, b_re, b_im, c_re, c_im, d_skip):
    bsz, s, _ = u.shape
    f32 = jnp.float32
    u32 = u.astype(f32).reshape(bsz, s, SSM_GROUPS, SSM_GROUP)
    dt = jnp.exp(log_dt.astype(f32))[:, None]
    ar = a_re.astype(f32)
    ai = a_im.astype(f32)
    mag = jnp.exp(dt * ar)
    abar_re = mag * jnp.cos(dt * ai)
    abar_im = mag * jnp.sin(dt * ai)
    den = ar * ar + ai * ai
    num_re = abar_re - 1.0
    zoh_re = (num_re * ar + abar_im * ai) / den
    zoh_im = (abar_im * ar - num_re * ai) / den
    br = b_re.astype(f32)
    bi = b_im.astype(f32)
    bbar_re = zoh_re[..., None] * br - zoh_im[..., None] * bi
    bbar_im = zoh_re[..., None] * bi + zoh_im[..., None] * br
    bu_re = jnp.einsum('bsgp,gnp->bsgn', u32, bbar_re)
    bu_im = jnp.einsum('bsgp,gnp->bsgn', u32, bbar_im)
    a_re_t = jnp.broadcast_to(abar_re, bu_re.shape)
    a_im_t = jnp.broadcast_to(abar_im, bu_im.shape)

    def combine(left, right):
        l_ar, l_ai, l_br, l_bi = left
        r_ar, r_ai, r_br, r_bi = right
        return (r_ar * l_ar - r_ai * l_ai,
                r_ar * l_ai + r_ai * l_ar,
                r_ar * l_br - r_ai * l_bi + r_br,
                r_ar * l_bi + r_ai * l_br + r_bi)

    _, _, x_re, x_im = lax.associative_scan(combine, (a_re_t, a_im_t, bu_re, bu_im), axis=1)
    y = (jnp.einsum('bsgn,gpn->bsgp', x_re, c_re.astype(f32))
         - jnp.einsum('bsgn,gpn->bsgp', x_im, c_im.astype(f32))
         + d_skip.astype(f32) * u32)
    return y.reshape(bsz, s, SSM_WIDTH).astype(u.dtype)


def hybrid_mixer(h, w_in, a_re, a_im, log_dt, b_re, b_im, c_re, c_im, d_skip,
                 w_glu, p_attn, p_ssm, w_out):
    b, s, _ = h.shape
    proj = h @ w_in
    cuts = [SB_WIDTH, 2 * SB_WIDTH, 3 * SB_WIDTH, 3 * SB_WIDTH + SSM_WIDTH,
            3 * SB_WIDTH + SSM_WIDTH + D_MODEL]
    q, k, v, u, gate_a, gate_b = jnp.split(proj, cuts, axis=-1)
    heads = lambda t: t.reshape(b, s, SB_HEADS, SB_HEAD_DIM)
    o_attn = stick_breaking_attention(heads(q), heads(k), heads(v))
    y = jax.nn.gelu(s5_ssm(u, a_re, a_im, log_dt, b_re, b_im, c_re, c_im, d_skip))
    o_ssm = y * jax.nn.sigmoid(y @ w_glu)
    merged = (jax.nn.sigmoid(gate_a) * (o_attn @ p_attn)
              + jax.nn.sigmoid(gate_b) * (o_ssm @ p_ssm))
    return merged @ w_out


def swiglu(h, w_gate, w_up, w_down):
    return (jax.nn.silu(h @ w_gate) * (h @ w_up)) @ w_down


def moe_swiglu(h, w_router, w_gate, w_up, w_down):
    logits = jnp.einsum('bsd,de->bse', h, w_router, preferred_element_type=jnp.float32)
    top_val, top_idx = lax.top_k(logits, TOP_K)
    gates = jax.nn.softmax(top_val, axis=-1)
    weights = jnp.sum(jax.nn.one_hot(top_idx, N_EXPERTS, dtype=jnp.float32)
                      * gates[..., None], axis=-2)
    out = jnp.zeros_like(h)
    for e in range(N_EXPERTS):
        out = out + weights[..., e:e + 1].astype(h.dtype) * swiglu(h, w_gate[e], w_up[e], w_down[e])
    return out


def setup_inputs(seed: int = 0) -> dict:
    key = jax.random.key(seed)
    ks = jax.random.split(key, 26)
    f32 = jnp.float32
    nrm = lambda k, shape, scale: jax.random.normal(k, shape, f32) * scale
    n_idx = jnp.arange(SSM_STATE, dtype=f32)
    a_re = -0.5 + nrm(ks[5], (DEPTH, SSM_GROUPS, SSM_STATE), 0.01)
    a_im = jnp.pi * n_idx + nrm(ks[6], (DEPTH, SSM_GROUPS, SSM_STATE), 0.01)
    log_dt = jax.random.uniform(ks[7], (DEPTH, SSM_GROUPS), f32,
                                float(np.log(DT_MIN)), float(np.log(DT_MAX)))
    b_scale = (2.0 * SSM_GROUP) ** -0.5
    c_scale = (2.0 * SSM_STATE) ** -0.5
    return {
        'x': nrm(ks[0], (BATCH, SEQ, D_MODEL), 1.0),
        'mix_norm': 1.0 + nrm(ks[1], (DEPTH, D_MODEL), 0.02),
        'ffn_norm': 1.0 + nrm(ks[2], (DEPTH, D_MODEL), 0.02),
        'final_norm': 1.0 + nrm(ks[3], (D_MODEL,), 0.02),
        'w_in': nrm(ks[4], (DEPTH, D_MODEL, IN_COLS), D_MODEL ** -0.5),
        'ssm_a_re': a_re,
        'ssm_a_im': a_im,
        'ssm_log_dt': log_dt,
        'ssm_b_re': nrm(ks[8], (DEPTH, SSM_GROUPS, SSM_STATE, SSM_GROUP), b_scale),
        'ssm_b_im': nrm(ks[9], (DEPTH, SSM_GROUPS, SSM_STATE, SSM_GROUP), b_scale),
        'ssm_c_re': nrm(ks[10], (DEPTH, SSM_GROUPS, SSM_GROUP, SSM_STATE), c_scale),
        'ssm_c_im': nrm(ks[11], (DEPTH, SSM_GROUPS, SSM_GROUP, SSM_STATE), c_scale),
        'ssm_d': nrm(ks[12], (DEPTH, SSM_GROUPS, SSM_GROUP), 1.0),
        'w_glu': nrm(ks[13], (DEPTH, SSM_WIDTH, SSM_WIDTH), SSM_WIDTH ** -0.5),
        'p_attn': nrm(ks[14], (DEPTH, SB_WIDTH, D_MODEL), SB_WIDTH ** -0.5),
        'p_ssm': nrm(ks[15], (DEPTH, SSM_WIDTH, D_MODEL), SSM_WIDTH ** -0.5),
        'w_out': nrm(ks[16], (DEPTH, D_MODEL, D_MODEL), D_MODEL ** -0.5),
        'ffn_w_gate': nrm(ks[17], (N_DENSE, D_MODEL, D_FF_DENSE), D_MODEL ** -0.5),
        'ffn_w_up': nrm(ks[18], (N_DENSE, D_MODEL, D_FF_DENSE), D_MODEL ** -0.5),
        'ffn_w_down': nrm(ks[19], (N_DENSE, D_FF_DENSE, D_MODEL), D_FF_DENSE ** -0.5),
        'w_router': nrm(ks[20], (N_MOE, D_MODEL, N_EXPERTS), D_MODEL ** -0.5),
        'moe_w_gate': nrm(ks[21], (N_MOE, N_EXPERTS, D_MODEL, D_FF_EXPERT), D_MODEL ** -0.5),
        'moe_w_up': nrm(ks[22], (N_MOE, N_EXPERTS, D_MODEL, D_FF_EXPERT), D_MODEL ** -0.5),
        'moe_w_down': nrm(ks[23], (N_MOE, N_EXPERTS, D_FF_EXPERT, D_MODEL), D_FF_EXPERT ** -0.5),
    }


def reference(x, mix_norm, ffn_norm, final_norm, w_in, ssm_a_re, ssm_a_im, ssm_log_dt,
              ssm_b_re, ssm_b_im, ssm_c_re, ssm_c_im, ssm_d, w_glu, p_attn, p_ssm, w_out,
              ffn_w_gate, ffn_w_up, ffn_w_down, w_router, moe_w_gate, moe_w_up, moe_w_down):
    for layer in range(DEPTH):
        h = rms_norm(x, mix_norm[layer])
        x = x + hybrid_mixer(h, w_in[layer], ssm_a_re[layer], ssm_a_im[layer], ssm_log_dt[layer],
                             ssm_b_re[layer], ssm_b_im[layer], ssm_c_re[layer], ssm_c_im[layer],
                             ssm_d[layer], w_glu[layer], p_attn[layer], p_ssm[layer], w_out[layer])
        h = rms_norm(x, ffn_norm[layer])
        i = layer // 2
        if layer % 2 == 0:
            x = x + swiglu(h, ffn_w_gate[i], ffn_w_up[i], ffn_w_down[i])
        else:
            x = x + moe_swiglu(h, w_router[i], moe_w_gate[i], moe_w_up[i], moe_w_down[i])
    return rms_norm(x, final_norm)
```

```python
import functools

import jax
import jax.numpy as jnp
from jax import lax
from jax.experimental import pallas as pl
from jax.experimental.pallas import tpu as pltpu

F32 = jnp.float32
BF16 = jnp.bfloat16

D_MODEL = 2048
SEQ = 16384
DEPTH = 2
HEAD_DIM = 128
HEADS = 8
SB_WIDTH = HEADS * HEAD_DIM
SSM_WIDTH = 1024
SSM_GROUP = 16
SSM_GROUPS = 64
SSM_STATE = 64
IN_COLS = 3 * SB_WIDTH + SSM_WIDTH + 2 * D_MODEL
D_FF_DENSE = 2 * D_MODEL
N_EXPERTS = 8
D_FF_EXPERT = D_MODEL // 2
RMS_EPS = 1e-6

LANES = 128
SUBLANES = 8
VMEM_LIMIT = 56 * 1024 * 1024

TQ = 256
TK = 256
CHUNK = 8
N_CHUNKS = SEQ // CHUNK
SLAB = LANES
N_SLABS = SSM_WIDTH // SLAB
SLAB_GROUPS = SLAB // SSM_GROUP
SLAB_STATE = SLAB_GROUPS * SSM_STATE
SSM_ROWS = 256


def _params(*sem):
    return pltpu.CompilerParams(dimension_semantics=sem, vmem_limit_bytes=VMEM_LIMIT)


def _rms_rows(x, g):
    ms = jnp.mean(x * x, axis=-1, keepdims=True)
    return x * lax.rsqrt(ms + RMS_EPS) * g


def _sigmoid(x):
    return 1.0 / (1.0 + jnp.exp(-x))


def _inproj_kernel(x_ref, g_ref, w_ref, p_ref, u_ref, h_ref):
    j = pl.program_id(1)

    @pl.when(j == 0)
    def _():
        h_ref[...] = _rms_rows(x_ref[...], g_ref[...]).astype(BF16)

    r = jnp.dot(h_ref[...], w_ref[...], preferred_element_type=F32).astype(BF16)
    p_ref[...] = r

    @pl.when(j == 3)
    def _():
        u_ref[...] = r


def _inproj(x, g, w):
    tm, tn = 512, 1024
    return pl.pallas_call(
        _inproj_kernel,
        grid=(SEQ // tm, IN_COLS // tn),
        in_specs=[
            pl.BlockSpec((tm, D_MODEL), lambda i, j: (i, 0)),
            pl.BlockSpec((1, D_MODEL), lambda i, j: (0, 0)),
            pl.BlockSpec((D_MODEL, tn), lambda i, j: (0, j)),
        ],
        out_specs=[
            pl.BlockSpec((tm, tn), lambda i, j: (i, j)),
            pl.BlockSpec((tm, SSM_WIDTH), lambda i, j: (i, 0)),
        ],
        out_shape=[
            jax.ShapeDtypeStruct((SEQ, IN_COLS), BF16),
            jax.ShapeDtypeStruct((SEQ, SSM_WIDTH), BF16),
        ],
        scratch_shapes=[pltpu.VMEM((tm, D_MODEL), BF16)],
        compiler_params=_params("parallel", "arbitrary"),
        name="inproj",
    )(x, g, w)


def _attn_kernel(q_ref, k_ref, vt_ref, o_ref):
    i = pl.program_id(1)
    qb = q_ref[...]
    row = lax.broadcasted_iota(jnp.int32, (TK, TQ), 0)
    col = lax.broadcasted_iota(jnp.int32, (TK, TQ), 1)
    upper = jnp.where(col >= row, 1.0, 0.0).astype(BF16)
    causal = row < col

    def block(j, carry, acc, masked):
        ks = pl.multiple_of(j * TK, TK)
        kb = k_ref[pl.ds(ks, TK), :]
        zt = lax.dot_general(kb, qb, (((1,), (1,)), ((), ())),
                             preferred_element_type=F32)
        sp = jnp.maximum(zt, 0.0) + jnp.log(1.0 + jnp.exp(-jnp.abs(zt)))
        if masked:
            sp = jnp.where(causal, sp, 0.0)
        rt = jnp.dot(upper, sp.astype(BF16), preferred_element_type=F32)
        w = jnp.exp(zt - rt - carry)
        if masked:
            w = jnp.where(causal, w, 0.0)
        acc = acc + jnp.dot(vt_ref[j], w.astype(BF16), preferred_element_type=F32)
        return carry + rt[0:1, :], acc

    carry = jnp.zeros((1, TQ), F32)
    acc = jnp.zeros((HEAD_DIM, TQ), F32)
    carry, acc = block(i, carry, acc, True)

    def body(n, c):
        return block(i - 1 - n, c[0], c[1], False)

    carry, acc = lax.fori_loop(0, i, body, (carry, acc))
    o_ref[...] = acc.T.astype(o_ref.dtype)


def _attention(proj, vt):
    return pl.pallas_call(
        _attn_kernel,
        grid=(HEADS, SEQ // TQ),
        in_specs=[
            pl.BlockSpec((TQ, HEAD_DIM), lambda h, i: (i, h)),
            pl.BlockSpec((SEQ, HEAD_DIM), lambda h, i: (0, HEADS + h)),
            pl.BlockSpec((None, SEQ // TK, HEAD_DIM, TK), lambda h, i: (h, 0, 0, 0)),
        ],
        out_specs=pl.BlockSpec((TQ, HEAD_DIM), lambda h, i: (i, h)),
        out_shape=jax.ShapeDtypeStruct((SEQ, SB_WIDTH), BF16),
        compiler_params=_params("parallel", "arbitrary"),
        name="attention",
    )(proj, proj, vt)


def _ssm_weights(a_re, a_im, log_dt, b_re, b_im, c_re, c_im, d_skip):
    dt = jnp.exp(log_dt)[:, None]
    d = jnp.arange(CHUNK + 1, dtype=F32)[:, None, None]
    mag = jnp.exp(d * (dt * a_re))
    pw_re = mag * jnp.cos(d * (dt * a_im))
    pw_im = mag * jnp.sin(d * (dt * a_im))
    abar_re, abar_im = pw_re[1], pw_im[1]
    den = a_re * a_re + a_im * a_im
    num_re = abar_re - 1.0
    zoh_re = (num_re * a_re + abar_im * a_im) / den
    zoh_im = (abar_im * a_re - num_re * a_im) / den
    bb_re = zoh_re[..., None] * b_re - zoh_im[..., None] * b_im
    bb_im = zoh_re[..., None] * b_im + zoh_im[..., None] * b_re
    ca_re = c_re[None] * pw_re[:, :, None, :] - c_im[None] * pw_im[:, :, None, :]
    ca_im = c_re[None] * pw_im[:, :, None, :] + c_im[None] * pw_re[:, :, None, :]
    kd = jnp.einsum('dgon,gnp->dgop', ca_re, bb_re) - jnp.einsum('dgon,gnp->dgop', ca_im, bb_im)
    kd = kd.at[0].add(jnp.eye(SSM_GROUP, dtype=F32)[None] * d_skip[:, :, None])
    lag = jnp.arange(CHUNK)[None, :] - jnp.arange(CHUNK)[:, None]
    tg = jnp.where((lag >= 0)[:, :, None, None, None], kd[jnp.clip(lag, 0, CHUNK)], 0.0)
    tg = tg.transpose(2, 0, 4, 1, 3)
    eye = jnp.eye(SLAB_GROUPS, dtype=F32)
    tg = tg.reshape(N_SLABS, SLAB_GROUPS, CHUNK, SSM_GROUP, CHUNK, SSM_GROUP)
    t_mat = jnp.einsum('kajpbq,ac->kjapbcq', tg, eye).reshape(N_SLABS, CHUNK * SLAB, CHUNK * SLAB)
    rev_re, rev_im = pw_re[CHUNK - 1::-1], pw_im[CHUNK - 1::-1]
    wz_re = rev_re[:, :, :, None] * bb_re[None] - rev_im[:, :, :, None] * bb_im[None]
    wz_im = rev_re[:, :, :, None] * bb_im[None] + rev_im[:, :, :, None] * bb_re[None]
    wz = jnp.stack([wz_re, wz_im], axis=0).transpose(2, 1, 4, 0, 3)
    wz = wz.reshape(N_SLABS, SLAB_GROUPS, CHUNK, SSM_GROUP, 2, SSM_STATE)
    wz_mat = jnp.einsum('kajprn,ac->kjaprcn', wz, eye).reshape(N_SLABS, CHUNK * SLAB, 2 * SLAB_STATE)
    v = jnp.stack([ca_re[1:], -ca_im[1:]], axis=0).transpose(2, 0, 4, 1, 3)
    v = v.reshape(N_SLABS, SLAB_GROUPS, 2, SSM_STATE, CHUNK, SSM_GROUP)
    v_mat = jnp.einsum('karnbq,ac->kranbcq', v, eye).reshape(N_SLABS, 2 * SLAB_STATE, CHUNK * SLAB)
    lam = jnp.stack([pw_re[CHUNK], pw_im[CHUNK]], axis=1)
    lam = lam.reshape(N_SLABS, SLAB_GROUPS, 2, SSM_STATE).transpose(0, 2, 1, 3)
    lam = lam.reshape(N_SLABS, 2, SLAB_STATE)
    return wz_mat.astype(BF16), t_mat.astype(BF16), v_mat.astype(BF16), lam


def _cmul(ar, ai, br, bi):
    return ar * br - ai * bi, ar * bi + ai * br


def _ssm_kernel(*refs):
    u_refs = refs[:CHUNK]
    wz_ref, t_ref, v_ref, lam_ref, o_ref, s_ref = refs[CHUNK:]
    ns = SLAB_STATE
    lam1 = (lam_ref[0:1, :], lam_ref[1:2, :])
    lam2 = _cmul(*lam1, *lam1)
    lam4 = _cmul(*lam2, *lam2)
    lam8 = _cmul(*lam4, *lam4)

    def lhs_rows(r0, n):
        return jnp.concatenate([u[pl.ds(r0, n), :] for u in u_refs], axis=1)

    def pass1(b, _):
        r0 = pl.multiple_of(b * SSM_ROWS, SSM_ROWS)
        z = jnp.dot(lhs_rows(r0, SSM_ROWS), wz_ref[...], preferred_element_type=F32)
        zr, zi = z[:, :ns], z[:, ns:]
        sub = lax.broadcasted_iota(jnp.int32, (SSM_ROWS, ns), 0) & (SUBLANES - 1)
        for shift, (cr, ci) in ((1, lam1), (2, lam2), (4, lam4)):
            keep = sub >= shift
            pr = jnp.where(keep, pltpu.roll(zr, shift, axis=0), 0.0)
            pi = jnp.where(keep, pltpu.roll(zi, shift, axis=0), 0.0)
            ar, ai = _cmul(cr, ci, pr, pi)
            zr, zi = zr + ar, zi + ai
        s_ref[pl.ds(r0, SSM_ROWS), :ns] = zr
        s_ref[pl.ds(r0, SSM_ROWS), ns:] = zi
        return 0

    lax.fori_loop(0, N_CHUNKS // SSM_ROWS, pass1, 0)

    pow_re, pow_im = [jnp.ones((1, ns), F32)], [jnp.zeros((1, ns), F32)]
    for _ in range(SUBLANES - 1):
        nr, ni = _cmul(pow_re[-1], pow_im[-1], *lam1)
        pow_re.append(nr)
        pow_im.append(ni)
    tab_re = jnp.concatenate(pow_re, axis=0)
    tab_im = jnp.concatenate(pow_im, axis=0)
    sub8 = lax.broadcasted_iota(jnp.int32, (SUBLANES, ns), 0)

    def pass2(t, c):
        cr, ci = c
        r0 = pl.multiple_of(t * SUBLANES, SUBLANES)
        lr = s_ref[pl.ds(r0, SUBLANES), :ns]
        li = s_ref[pl.ds(r0, SUBLANES), ns:]
        er = jnp.where(sub8 >= 1, pltpu.roll(lr, 1, axis=0), 0.0)
        ei = jnp.where(sub8 >= 1, pltpu.roll(li, 1, axis=0), 0.0)
        ar, ai = _cmul(tab_re, tab_im, cr, ci)
        s_ref[pl.ds(r0, SUBLANES), :ns] = er + ar
        s_ref[pl.ds(r0, SUBLANES), ns:] = ei + ai
        nr, ni = _cmul(*lam8, cr, ci)
        return lr[SUBLANES - 1:SUBLANES, :] + nr, li[SUBLANES - 1:SUBLANES, :] + ni

    zero = jnp.zeros((1, ns), F32)
    lax.fori_loop(0, N_CHUNKS // SUBLANES, pass2, (zero, zero))

    def pass3(b, _):
        r0 = pl.multiple_of(b * SSM_ROWS, SSM_ROWS)
        y = jnp.dot(lhs_rows(r0, SSM_ROWS), t_ref[...], preferred_element_type=F32)
        y = y + jnp.dot(s_ref[pl.ds(r0, SSM_ROWS), :].astype(BF16), v_ref[...],
                        preferred_element_type=F32)
        o_ref[pl.ds(r0, SSM_ROWS), :] = y.astype(o_ref.dtype)
        return 0

    lax.fori_loop(0, N_CHUNKS // SSM_ROWS, pass3, 0)


def _ssm(u, wz_mat, t_mat, v_mat, lam):
    u2 = u.reshape(N_CHUNKS, CHUNK * SSM_WIDTH)
    u_specs = [pl.BlockSpec((N_CHUNKS, SLAB), functools.partial(lambda j, k: (0, j * N_SLABS + k), j))
               for j in range(CHUNK)]
    yk = pl.pallas_call(
        _ssm_kernel,
        grid=(N_SLABS,),
        in_specs=u_specs + [
            pl.BlockSpec((None, CHUNK * SLAB, 2 * SLAB_STATE), lambda k: (k, 0, 0)),
            pl.BlockSpec((None, CHUNK * SLAB, CHUNK * SLAB), lambda k: (k, 0, 0)),
            pl.BlockSpec((None, 2 * SLAB_STATE, CHUNK * SLAB), lambda k: (k, 0, 0)),
            pl.BlockSpec((None, 2, SLAB_STATE), lambda k: (k, 0, 0)),
        ],
        out_specs=pl.BlockSpec((None, N_CHUNKS, CHUNK * SLAB), lambda k: (k, 0, 0)),
        out_shape=jax.ShapeDtypeStruct((N_SLABS, N_CHUNKS, CHUNK * SLAB), BF16),
        scratch_shapes=[pltpu.VMEM((N_CHUNKS, 2 * SLAB_STATE), F32)],
        compiler_params=_params("arbitrary"),
        name="ssm",
    )(*([u2] * CHUNK), wz_mat, t_mat, v_mat, lam)
    return yk.reshape(N_SLABS, N_CHUNKS, CHUNK, SLAB).transpose(1, 2, 0, 3).reshape(SEQ, SSM_WIDTH)


def _gelu_tanh(x):
    c = 0.7978845608028654
    return 0.5 * x * (1.0 + jnp.tanh(c * (x + 0.044715 * (x * x * x))))


def _post_kernel(x_ref, oa_ref, y_ref, ga_ref, gb_ref, wglu_ref, pa_ref, ps_ref, wo_ref, o_ref):
    y = _gelu_tanh(y_ref[...].astype(F32))
    yb = y.astype(BF16)
    glu = jnp.dot(yb, wglu_ref[...], preferred_element_type=F32)
    o_ssm = (y * _sigmoid(glu)).astype(BF16)
    a = jnp.dot(oa_ref[...], pa_ref[...], preferred_element_type=F32)
    b = jnp.dot(o_ssm, ps_ref[...], preferred_element_type=F32)
    merged = _sigmoid(ga_ref[...].astype(F32)) * a + _sigmoid(gb_ref[...].astype(F32)) * b
    o_ref[...] = x_ref[...] + jnp.dot(merged.astype(BF16), wo_ref[...], preferred_element_type=F32)


def _post(x, o_attn, y, proj, w_glu, p_attn, p_ssm, w_out):
    tm = 256
    ga_blk = (3 * SB_WIDTH + SSM_WIDTH) // D_MODEL
    const = lambda shape: pl.BlockSpec(shape, lambda i: (0, 0), pipeline_mode=pl.Buffered(1))
    return pl.pallas_call(
        _post_kernel,
        grid=(SEQ // tm,),
        in_specs=[
            pl.BlockSpec((tm, D_MODEL), lambda i: (i, 0)),
            pl.BlockSpec((tm, SB_WIDTH), lambda i: (i, 0)),
            pl.BlockSpec((tm, SSM_WIDTH), lambda i: (i, 0)),
            pl.BlockSpec((tm, D_MODEL), lambda i: (i, ga_blk)),
            pl.BlockSpec((tm, D_MODEL), lambda i: (i, ga_blk + 1)),
            const((SSM_WIDTH, SSM_WIDTH)),
            const((SB_WIDTH, D_MODEL)),
            const((SSM_WIDTH, D_MODEL)),
            const((D_MODEL, D_MODEL)),
        ],
        out_specs=pl.BlockSpec((tm, D_MODEL), lambda i: (i, 0)),
        out_shape=jax.ShapeDtypeStruct((SEQ, D_MODEL), F32),
        compiler_params=_params("parallel"),
        name="post",
    )(x, o_attn, y, proj, proj, w_glu, p_attn, p_ssm, w_out)


def _ffn_kernel(x_ref, g_ref, wg_ref, wu_ref, wd_ref, o_ref, h_ref):
    f = pl.program_id(1)

    @pl.when(f == 0)
    def _():
        x = x_ref[...]
        h_ref[...] = _rms_rows(x, g_ref[...]).astype(BF16)
        o_ref[...] = x

    h = h_ref[...]
    a = jnp.dot(h, wg_ref[...], preferred_element_type=F32)
    b = jnp.dot(h, wu_ref[...], preferred_element_type=F32)
    t = (a * _sigmoid(a) * b).astype(BF16)
    o_ref[...] += jnp.dot(t, wd_ref[...], preferred_element_type=F32)


def _ffn(x, g, w_gate, w_up, w_down):
    tm, tf = 512, 512
    return pl.pallas_call(
        _ffn_kernel,
        grid=(SEQ // tm, D_FF_DENSE // tf),
        in_specs=[
            pl.BlockSpec((tm, D_MODEL), lambda i, f: (i, 0)),
            pl.BlockSpec((1, D_MODEL), lambda i, f: (0, 0)),
            pl.BlockSpec((D_MODEL, tf), lambda i, f: (0, f)),
            pl.BlockSpec((D_MODEL, tf), lambda i, f: (0, f)),
            pl.BlockSpec((tf, D_MODEL), lambda i, f: (f, 0)),
        ],
        out_specs=pl.BlockSpec((tm, D_MODEL), lambda i, f: (i, 0)),
        out_shape=jax.ShapeDtypeStruct((SEQ, D_MODEL), F32),
        scratch_shapes=[pltpu.VMEM((tm, D_MODEL), BF16)],
        compiler_params=_params("parallel", "arbitrary"),
        name="ffn",
    )(x, g, w_gate, w_up, w_down)


def _moe_kernel(x_ref, g_ref, wr_ref, wg_ref, wu_ref, wd_ref, gf_ref, o_ref, h_ref, rw_ref, acc_ref):
    e = pl.program_id(1)
    f = pl.program_id(2)
    first = jnp.logical_and(e == 0, f == 0)
    last = jnp.logical_and(e == pl.num_programs(1) - 1, f == pl.num_programs(2) - 1)
    tm = x_ref.shape[0]
    lane = lax.broadcasted_iota(jnp.int32, (tm, LANES), 1)

    @pl.when(first)
    def _():
        x = x_ref[...]
        hf = _rms_rows(x, g_ref[...])
        h_ref[...] = hf.astype(BF16)
        acc_ref[...] = x
        logits = jnp.dot(hf, wr_ref[...], preferred_element_type=F32, precision=lax.Precision.HIGHEST)
        lg = jnp.where(lane < N_EXPERTS, logits, -jnp.inf)
        m1 = jnp.max(lg, axis=-1, keepdims=True)
        i1 = jnp.min(jnp.where(lg == m1, lane, LANES), axis=-1, keepdims=True)
        lg2 = jnp.where(lane == i1, -jnp.inf, lg)
        m2 = jnp.max(lg2, axis=-1, keepdims=True)
        i2 = jnp.min(jnp.where(lg2 == m2, lane, LANES), axis=-1, keepdims=True)
        e2 = jnp.exp(m2 - m1)
        g1 = 1.0 / (1.0 + e2)
        g2 = e2 / (1.0 + e2)
        rw_ref[...] = jnp.where(lane == i1, g1, 0.0) + jnp.where(lane == i2, g2, 0.0)

    h = h_ref[...]
    we = jnp.sum(jnp.where(lane == e, rw_ref[...], 0.0), axis=-1, keepdims=True)
    a = jnp.dot(h, wg_ref[...], preferred_element_type=F32)
    b = jnp.dot(h, wu_ref[...], preferred_element_type=F32)
    t = (a * _sigmoid(a) * b * we).astype(BF16)
    acc_ref[...] += jnp.dot(t, wd_ref[...], preferred_element_type=F32)

    @pl.when(last)
    def _():
        o_ref[...] = _rms_rows(acc_ref[...], gf_ref[...])


def _moe(x, g, w_router, w_gate, w_up, w_down, g_final):
    tm, tf = 512, 512
    return pl.pallas_call(
        _moe_kernel,
        grid=(SEQ // tm, N_EXPERTS, D_FF_EXPERT // tf),
        in_specs=[
            pl.BlockSpec((tm, D_MODEL), lambda i, e, f: (i, 0)),
            pl.BlockSpec((1, D_MODEL), lambda i, e, f: (0, 0)),
            pl.BlockSpec((D_MODEL, LANES), lambda i, e, f: (0, 0)),
            pl.BlockSpec((None, D_MODEL, tf), lambda i, e, f: (e, 0, f)),
            pl.BlockSpec((None, D_MODEL, tf), lambda i, e, f: (e, 0, f)),
            pl.BlockSpec((None, tf, D_MODEL), lambda i, e, f: (e, f, 0)),
            pl.BlockSpec((1, D_MODEL), lambda i, e, f: (0, 0)),
        ],
        out_specs=pl.BlockSpec((tm, D_MODEL), lambda i, e, f: (i, 0)),
        out_shape=jax.ShapeDtypeStruct((SEQ, D_MODEL), F32),
        scratch_shapes=[
            pltpu.VMEM((tm, D_MODEL), BF16),
            pltpu.VMEM((tm, LANES), F32),
            pltpu.VMEM((tm, D_MODEL), F32),
        ],
        compiler_params=_params("parallel", "arbitrary", "arbitrary"),
        name="moe",
    )(x, g, w_router, w_gate, w_up, w_down, g_final)


def kernel(x, mix_norm, ffn_norm, final_norm, w_in, ssm_a_re, ssm_a_im, ssm_log_dt, ssm_b_re, ssm_b_im,
           ssm_c_re, ssm_c_im, ssm_d, w_glu, p_attn, p_ssm, w_out, ffn_w_gate, ffn_w_up, ffn_w_down,
           w_router, moe_w_gate, moe_w_up, moe_w_down):
    assert x.shape == (1, SEQ, D_MODEL) and DEPTH == 2
    xs = x.reshape(SEQ, D_MODEL)
    col_scale = jnp.concatenate([jnp.full((SB_WIDTH,), HEAD_DIM ** -0.5, F32),
                                 jnp.ones((IN_COLS - SB_WIDTH,), F32)])
    for layer in range(DEPTH):
        w_in_b = (w_in[layer] * col_scale).astype(BF16)
        proj, u = _inproj(xs, mix_norm[layer][None], w_in_b)
        v = proj[:, 2 * SB_WIDTH:3 * SB_WIDTH]
        vt = v.reshape(SEQ // TK, TK, HEADS, HEAD_DIM).transpose(2, 0, 3, 1)
        o_attn = _attention(proj, vt)
        ssm_w = _ssm_weights(ssm_a_re[layer], ssm_a_im[layer], ssm_log_dt[layer], ssm_b_re[layer],
                             ssm_b_im[layer], ssm_c_re[layer], ssm_c_im[layer], ssm_d[layer])
        y = _ssm(u, *ssm_w)
        xs = _post(xs, o_attn, y, proj, w_glu[layer].astype(BF16), p_attn[layer].astype(BF16),
                   p_ssm[layer].astype(BF16), w_out[layer].astype(BF16))
        i = layer // 2
        if layer % 2 == 0:
            xs = _ffn(xs, ffn_norm[layer][None], ffn_w_gate[i].astype(BF16), ffn_w_up[i].astype(BF16),
                      ffn_w_down[i].astype(BF16))
        else:
            wr = jnp.pad(w_router[i], ((0, 0), (0, LANES - N_EXPERTS)))
            xs = _moe(xs, ffn_norm[layer][None], wr, moe_w_gate[i].astype(BF16), moe_w_up[i].astype(BF16),
                      moe_w_down[i].astype(BF16), final_norm[None])
    return xs.reshape(1, SEQ, D_MODEL)
```

```python
import functools

import jax
import jax.numpy as jnp
from jax import lax
from jax.experimental import pallas as pl
from jax.experimental.pallas import tpu as pltpu

F32 = jnp.float32
BF16 = jnp.bfloat16

D_MODEL = 2048
SEQ = 16384
DEPTH = 2
HEAD_DIM = 128
HEADS = 8
SB_WIDTH = HEADS * HEAD_DIM
SSM_WIDTH = 1024
SSM_GROUP = 16
SSM_GROUPS = 64
SSM_STATE = 64
IN_COLS = 3 * SB_WIDTH + SSM_WIDTH + 2 * D_MODEL
D_FF_DENSE = 2 * D_MODEL
N_EXPERTS = 8
D_FF_EXPERT = D_MODEL // 2
RMS_EPS = 1e-6

LANES = 128
SUBLANES = 8
VMEM_LIMIT = 56 * 1024 * 1024

TQ = 256
TK = 256
HP = 4
LOG2E = 1.4426950408889634
UNDERFLOW_LOG2 = 152.0
NORM_MARGIN = 1.01
CHUNK = 8
N_CHUNKS = SEQ // CHUNK
SLAB = LANES
N_SLABS = SSM_WIDTH // SLAB
SLAB_GROUPS = SLAB // SSM_GROUP
SLAB_STATE = SLAB_GROUPS * SSM_STATE
SSM_ROWS = 256


def _params(*sem):
    return pltpu.CompilerParams(dimension_semantics=sem, vmem_limit_bytes=VMEM_LIMIT)


def _rms_rows(x, g):
    ms = jnp.mean(x * x, axis=-1, keepdims=True)
    return x * lax.rsqrt(ms + RMS_EPS) * g


def _sigmoid(x):
    return 1.0 / (1.0 + jnp.exp(-x))


def _inproj_kernel(x_ref, g_ref, w_ref, p_ref, u_ref, vt_ref, h_ref):
    j = pl.program_id(1)

    @pl.when(j == 0)
    def _():
        h_ref[...] = _rms_rows(x_ref[...], g_ref[...]).astype(BF16)

    r = jnp.dot(h_ref[...], w_ref[...], preferred_element_type=F32)
    p_ref[...] = r.astype(BF16)

    @pl.when(j == 2)
    def _():
        rt = r.T
        for h in range(HEADS):
            for b in range(vt_ref.shape[1]):
                vt_ref[h, b] = rt[h * HEAD_DIM:(h + 1) * HEAD_DIM, b * TK:(b + 1) * TK].astype(BF16)

    @pl.when(j == 3)
    def _():
        u_ref[...] = r.astype(BF16)


def _inproj(x, g, w):
    tm, tn = 512, 1024
    assert tn == SB_WIDTH == SSM_WIDTH and tm % TK == 0
    return pl.pallas_call(
        _inproj_kernel,
        grid=(SEQ // tm, IN_COLS // tn),
        in_specs=[
            pl.BlockSpec((tm, D_MODEL), lambda i, j: (i, 0)),
            pl.BlockSpec((1, D_MODEL), lambda i, j: (0, 0)),
            pl.BlockSpec((D_MODEL, tn), lambda i, j: (0, j)),
        ],
        out_specs=[
            pl.BlockSpec((tm, tn), lambda i, j: (i, j)),
            pl.BlockSpec((tm, SSM_WIDTH), lambda i, j: (i, 0)),
            pl.BlockSpec((HEADS, tm // TK, HEAD_DIM, TK), lambda i, j: (0, i, 0, 0)),
        ],
        out_shape=[
            jax.ShapeDtypeStruct((SEQ, IN_COLS), BF16),
            jax.ShapeDtypeStruct((SEQ, SSM_WIDTH), BF16),
            jax.ShapeDtypeStruct((HEADS, SEQ // TK, HEAD_DIM, TK), BF16),
        ],
        scratch_shapes=[pltpu.VMEM((tm, D_MODEL), BF16)],
        compiler_params=_params("parallel", "arbitrary"),
        name="inproj",
    )(x, g, w)


def _attn_kernel(q_ref, k_ref, vt_ref, o_ref, kmax_ref, carry_ref, acc_ref):
    i = pl.program_id(1)
    hs = lambda h: slice(h * HEAD_DIM, (h + 1) * HEAD_DIM)

    @pl.when(i == 0)
    def _():
        def body(j, run):
            ks = pl.multiple_of(j * TK, TK)
            new, rows = [], []
            for h in range(HP):
                kb = k_ref[pl.ds(ks, TK), hs(h)].astype(F32)
                n2 = jnp.max(jnp.sum(kb * kb, axis=1, keepdims=True), axis=0, keepdims=True)
                new.append(jnp.maximum(run[h], n2))
                rows.append(jnp.broadcast_to(jnp.sqrt(new[h]), (1, TQ)))
            kmax_ref[j] = jnp.concatenate(rows, axis=0)
            return tuple(new)

        lax.fori_loop(0, SEQ // TK, body, tuple(jnp.zeros((1, 1), F32) for _ in range(HP)))

    row = lax.broadcasted_iota(jnp.int32, (TK, TQ), 0)
    col = lax.broadcasted_iota(jnp.int32, (TK, TQ), 1)
    upper = jnp.where(col >= row, 1.0, 0.0).astype(BF16)
    causal = row < col

    def block(h, j, carry, acc, masked):
        ks = pl.multiple_of(j * TK, TK)
        kb = k_ref[pl.ds(ks, TK), hs(h)]
        zt = lax.dot_general(kb, q_ref[:, hs(h)], (((1,), (1,)), ((), ())),
                             preferred_element_type=F32)
        neg_abs = lax.bitcast_convert_type(
            lax.bitcast_convert_type(zt, jnp.uint32) | jnp.uint32(0x80000000), F32)
        sp = jnp.maximum(zt, 0.0) + jnp.log(1.0 + jnp.exp2(neg_abs)) * LOG2E
        if masked:
            sp = jnp.where(causal, sp, 0.0)
        rt = jnp.dot(upper, sp.astype(BF16), preferred_element_type=F32)
        w = jnp.exp2(zt - rt - carry)
        if masked:
            w = jnp.where(causal, w, 0.0)
        acc = acc + jnp.dot(vt_ref[h, j], w.astype(BF16), preferred_element_type=F32)
        return carry + rt[0:1, :], acc

    prev = jnp.maximum(i - 1, 0)
    no_prev = jnp.where(i == 0, 1e30, 0.0).astype(F32)
    qmax = []
    for h in range(HP):
        carry, acc = block(h, i, jnp.zeros((1, TQ), F32), jnp.zeros((HEAD_DIM, TQ), F32), True)
        carry, acc = block(h, prev, carry + no_prev, acc, False)
        carry_ref[h] = carry
        acc_ref[h] = acc
        qf = q_ref[:, hs(h)].astype(F32)
        n2 = jnp.max(jnp.sum(qf * qf, axis=1, keepdims=True), axis=0, keepdims=True)
        qmax.append(jnp.sqrt(n2) * NORM_MARGIN)

    def more_needed(j):
        km = kmax_ref[j]
        worst = None
        for h in range(HP):
            slack = carry_ref[h] - (UNDERFLOW_LOG2 + qmax[h] * km[h:h + 1, :])
            worst = slack if worst is None else jnp.minimum(worst, slack)
        return (jnp.min(worst) < 0.0).astype(jnp.int32)

    def cond(c):
        return jnp.logical_and(c[0] >= 0, c[1] > 0)

    def body(c):
        j = c[0]
        for h in range(HP):
            carry, acc = block(h, j, carry_ref[h], acc_ref[h], False)
            carry_ref[h] = carry
            acc_ref[h] = acc
        return j - 1, more_needed(jnp.maximum(j - 1, 0))

    lax.while_loop(cond, body, (i - 2, more_needed(jnp.maximum(i - 2, 0))))
    for h in range(HP):
        o_ref[:, hs(h)] = acc_ref[h].T.astype(o_ref.dtype)


def _attention(proj, vt):
    wide = HP * HEAD_DIM
    once = pl.Buffered(1)
    return pl.pallas_call(
        _attn_kernel,
        grid=(HEADS // HP, SEQ // TQ),
        in_specs=[
            pl.BlockSpec((TQ, wide), lambda g, i: (i, g)),
            pl.BlockSpec((SEQ, wide), lambda g, i: (0, HEADS // HP + g), pipeline_mode=once),
            pl.BlockSpec((HP, SEQ // TK, HEAD_DIM, TK), lambda g, i: (g, 0, 0, 0), pipeline_mode=once),
        ],
        out_specs=pl.BlockSpec((TQ, wide), lambda g, i: (i, g)),
        out_shape=jax.ShapeDtypeStruct((SEQ, SB_WIDTH), BF16),
        scratch_shapes=[
            pltpu.VMEM((SEQ // TK, HP, TQ), F32),
            pltpu.VMEM((HP, 1, TQ), F32),
            pltpu.VMEM((HP, HEAD_DIM, TQ), F32),
        ],
        compiler_params=_params("arbitrary", "arbitrary"),
        name="attention",
    )(proj, proj, vt)


def _ssm_weights(a_re, a_im, log_dt, b_re, b_im, c_re, c_im, d_skip):
    dt = jnp.exp(log_dt)[:, None]
    d = jnp.arange(CHUNK + 1, dtype=F32)[:, None, None]
    mag = jnp.exp(d * (dt * a_re))
    pw_re = mag * jnp.cos(d * (dt * a_im))
    pw_im = mag * jnp.sin(d * (dt * a_im))
    abar_re, abar_im = pw_re[1], pw_im[1]
    den = a_re * a_re + a_im * a_im
    num_re = abar_re - 1.0
    zoh_re = (num_re * a_re + abar_im * a_im) / den
    zoh_im = (abar_im * a_re - num_re * a_im) / den
    bb_re = zoh_re[..., None] * b_re - zoh_im[..., None] * b_im
    bb_im = zoh_re[..., None] * b_im + zoh_im[..., None] * b_re
    ca_re = c_re[None] * pw_re[:, :, None, :] - c_im[None] * pw_im[:, :, None, :]
    ca_im = c_re[None] * pw_im[:, :, None, :] + c_im[None] * pw_re[:, :, None, :]
    kd = jnp.einsum('dgon,gnp->dgop', ca_re, bb_re) - jnp.einsum('dgon,gnp->dgop', ca_im, bb_im)
    kd = kd.at[0].add(jnp.eye(SSM_GROUP, dtype=F32)[None] * d_skip[:, :, None])
    lag = jnp.arange(CHUNK)[None, :] - jnp.arange(CHUNK)[:, None]
    tg = jnp.where((lag >= 0)[:, :, None, None, None], kd[jnp.clip(lag, 0, CHUNK)], 0.0)
    tg = tg.transpose(2, 0, 4, 1, 3)
    eye = jnp.eye(SLAB_GROUPS, dtype=F32)
    tg = tg.reshape(N_SLABS, SLAB_GROUPS, CHUNK, SSM_GROUP, CHUNK, SSM_GROUP)
    t_mat = jnp.einsum('kajpbq,ac->kjapbcq', tg, eye).reshape(N_SLABS, CHUNK * SLAB, CHUNK * SLAB)
    rev_re, rev_im = pw_re[CHUNK - 1::-1], pw_im[CHUNK - 1::-1]
    wz_re = rev_re[:, :, :, None] * bb_re[None] - rev_im[:, :, :, None] * bb_im[None]
    wz_im = rev_re[:, :, :, None] * bb_im[None] + rev_im[:, :, :, None] * bb_re[None]
    wz = jnp.stack([wz_re, wz_im], axis=0).transpose(2, 1, 4, 0, 3)
    wz = wz.reshape(N_SLABS, SLAB_GROUPS, CHUNK, SSM_GROUP, 2, SSM_STATE)
    wz_mat = jnp.einsum('kajprn,ac->kjaprcn', wz, eye).reshape(N_SLABS, CHUNK * SLAB, 2 * SLAB_STATE)
    v = jnp.stack([ca_re[1:], -ca_im[1:]], axis=0).transpose(2, 0, 4, 1, 3)
    v = v.reshape(N_SLABS, SLAB_GROUPS, 2, SSM_STATE, CHUNK, SSM_GROUP)
    v_mat = jnp.einsum('karnbq,ac->kranbcq', v, eye).reshape(N_SLABS, 2 * SLAB_STATE, CHUNK * SLAB)
    lam = jnp.stack([pw_re[CHUNK], pw_im[CHUNK]], axis=1)
    lam = lam.reshape(N_SLABS, SLAB_GROUPS, 2, SSM_STATE).transpose(0, 2, 1, 3)
    lam = lam.reshape(N_SLABS, 2, SLAB_STATE)
    return wz_mat.astype(BF16), t_mat.astype(BF16), v_mat.astype(BF16), lam


def _cmul(ar, ai, br, bi):
    return ar * br - ai * bi, ar * bi + ai * br


def _ssm_kernel(*refs):
    u_refs = refs[:CHUNK]
    wz_ref, t_ref, v_ref, lam_ref, o_ref, s_ref = refs[CHUNK:]
    ns = SLAB_STATE
    lam1 = (lam_ref[0:1, :], lam_ref[1:2, :])
    lam2 = _cmul(*lam1, *lam1)
    lam4 = _cmul(*lam2, *lam2)
    lam8 = _cmul(*lam4, *lam4)

    def lhs_rows(r0, n):
        return jnp.concatenate([u[pl.ds(r0, n), :] for u in u_refs], axis=1)

    def pass1(b, _):
        r0 = pl.multiple_of(b * SSM_ROWS, SSM_ROWS)
        z = jnp.dot(lhs_rows(r0, SSM_ROWS), wz_ref[...], preferred_element_type=F32)
        zr, zi = z[:, :ns], z[:, ns:]
        sub = lax.broadcasted_iota(jnp.int32, (SSM_ROWS, ns), 0) & (SUBLANES - 1)
        for shift, (cr, ci) in ((1, lam1), (2, lam2), (4, lam4)):
            keep = sub >= shift
            pr = jnp.where(keep, pltpu.roll(zr, shift, axis=0), 0.0)
            pi = jnp.where(keep, pltpu.roll(zi, shift, axis=0), 0.0)
            ar, ai = _cmul(cr, ci, pr, pi)
            zr, zi = zr + ar, zi + ai
        s_ref[pl.ds(r0, SSM_ROWS), :ns] = zr
        s_ref[pl.ds(r0, SSM_ROWS), ns:] = zi
        return 0

    lax.fori_loop(0, N_CHUNKS // SSM_ROWS, pass1, 0)

    pow_re, pow_im = [jnp.ones((1, ns), F32)], [jnp.zeros((1, ns), F32)]
    for _ in range(SUBLANES - 1):
        nr, ni = _cmul(pow_re[-1], pow_im[-1], *lam1)
        pow_re.append(nr)
        pow_im.append(ni)
    tab_re = jnp.concatenate(pow_re, axis=0)
    tab_im = jnp.concatenate(pow_im, axis=0)
    sub8 = lax.broadcasted_iota(jnp.int32, (SUBLANES, ns), 0)

    def pass2(t, c):
        cr, ci = c
        r0 = pl.multiple_of(t * SUBLANES, SUBLANES)
        lr = s_ref[pl.ds(r0, SUBLANES), :ns]
        li = s_ref[pl.ds(r0, SUBLANES), ns:]
        er = jnp.where(sub8 >= 1, pltpu.roll(lr, 1, axis=0), 0.0)
        ei = jnp.where(sub8 >= 1, pltpu.roll(li, 1, axis=0), 0.0)
        ar, ai = _cmul(tab_re, tab_im, cr, ci)
        s_ref[pl.ds(r0, SUBLANES), :ns] = er + ar
        s_ref[pl.ds(r0, SUBLANES), ns:] = ei + ai
        nr, ni = _cmul(*lam8, cr, ci)
        return lr[SUBLANES - 1:SUBLANES, :] + nr, li[SUBLANES - 1:SUBLANES, :] + ni

    zero = jnp.zeros((1, ns), F32)
    lax.fori_loop(0, N_CHUNKS // SUBLANES, pass2, (zero, zero))

    def pass3(b, _):
        r0 = pl.multiple_of(b * SSM_ROWS, SSM_ROWS)
        y = jnp.dot(lhs_rows(r0, SSM_ROWS), t_ref[...], preferred_element_type=F32)
        y = y + jnp.dot(s_ref[pl.ds(r0, SSM_ROWS), :].astype(BF16), v_ref[...],
                        preferred_element_type=F32)
        o_ref[pl.ds(r0, SSM_ROWS), :] = y.astype(o_ref.dtype)
        return 0

    lax.fori_loop(0, N_CHUNKS // SSM_ROWS, pass3, 0)


def _ssm(u, wz_mat, t_mat, v_mat, lam):
    u2 = u.reshape(N_CHUNKS, CHUNK * SSM_WIDTH)
    u_specs = [pl.BlockSpec((N_CHUNKS, SLAB), functools.partial(lambda j, k: (0, j * N_SLABS + k), j))
               for j in range(CHUNK)]
    yk = pl.pallas_call(
        _ssm_kernel,
        grid=(N_SLABS,),
        in_specs=u_specs + [
            pl.BlockSpec((None, CHUNK * SLAB, 2 * SLAB_STATE), lambda k: (k, 0, 0)),
            pl.BlockSpec((None, CHUNK * SLAB, CHUNK * SLAB), lambda k: (k, 0, 0)),
            pl.BlockSpec((None, 2 * SLAB_STATE, CHUNK * SLAB), lambda k: (k, 0, 0)),
            pl.BlockSpec((None, 2, SLAB_STATE), lambda k: (k, 0, 0)),
        ],
        out_specs=pl.BlockSpec((None, N_CHUNKS, CHUNK * SLAB), lambda k: (k, 0, 0)),
        out_shape=jax.ShapeDtypeStruct((N_SLABS, N_CHUNKS, CHUNK * SLAB), BF16),
        scratch_shapes=[pltpu.VMEM((N_CHUNKS, 2 * SLAB_STATE), F32)],
        compiler_params=_params("arbitrary"),
        name="ssm",
    )(*([u2] * CHUNK), wz_mat, t_mat, v_mat, lam)
    return yk.reshape(N_SLABS, N_CHUNKS, CHUNK, SLAB).transpose(1, 2, 0, 3).reshape(SEQ, SSM_WIDTH)


def _gelu_tanh(x):
    c = 0.7978845608028654
    return 0.5 * x * (1.0 + jnp.tanh(c * (x + 0.044715 * (x * x * x))))


def _post_kernel(x_ref, oa_ref, y_ref, ga_ref, gb_ref, wglu_ref, pa_ref, ps_ref, wo_ref, o_ref):
    y = _gelu_tanh(y_ref[...].astype(F32))
    yb = y.astype(BF16)
    glu = jnp.dot(yb, wglu_ref[...], preferred_element_type=F32)
    o_ssm = (y * _sigmoid(glu)).astype(BF16)
    a = jnp.dot(oa_ref[...], pa_ref[...], preferred_element_type=F32)
    b = jnp.dot(o_ssm, ps_ref[...], preferred_element_type=F32)
    merged = _sigmoid(ga_ref[...].astype(F32)) * a + _sigmoid(gb_ref[...].astype(F32)) * b
    o_ref[...] = x_ref[...] + jnp.dot(merged.astype(BF16), wo_ref[...], preferred_element_type=F32)


def _post(x, o_attn, y, proj, w_glu, p_attn, p_ssm, w_out):
    tm = 256
    ga_blk = (3 * SB_WIDTH + SSM_WIDTH) // D_MODEL
    const = lambda shape: pl.BlockSpec(shape, lambda i: (0, 0), pipeline_mode=pl.Buffered(1))
    return pl.pallas_call(
        _post_kernel,
        grid=(SEQ // tm,),
        in_specs=[
            pl.BlockSpec((tm, D_MODEL), lambda i: (i, 0)),
            pl.BlockSpec((tm, SB_WIDTH), lambda i: (i, 0)),
            pl.BlockSpec((tm, SSM_WIDTH), lambda i: (i, 0)),
            pl.BlockSpec((tm, D_MODEL), lambda i: (i, ga_blk)),
            pl.BlockSpec((tm, D_MODEL), lambda i: (i, ga_blk + 1)),
            const((SSM_WIDTH, SSM_WIDTH)),
            const((SB_WIDTH, D_MODEL)),
            const((SSM_WIDTH, D_MODEL)),
            const((D_MODEL, D_MODEL)),
        ],
        out_specs=pl.BlockSpec((tm, D_MODEL), lambda i: (i, 0)),
        out_shape=jax.ShapeDtypeStruct((SEQ, D_MODEL), F32),
        compiler_params=_params("parallel"),
        name="post",
    )(x, o_attn, y, proj, proj, w_glu, p_attn, p_ssm, w_out)


def _ffn_kernel(x_ref, g_ref, wg_ref, wu_ref, wd_ref, o_ref, h_ref):
    f = pl.program_id(1)

    @pl.when(f == 0)
    def _():
        x = x_ref[...]
        h_ref[...] = _rms_rows(x, g_ref[...]).astype(BF16)
        o_ref[...] = x

    h = h_ref[...]
    a = jnp.dot(h, wg_ref[...], preferred_element_type=F32)
    b = jnp.dot(h, wu_ref[...], preferred_element_type=F32)
    t = (a * _sigmoid(a) * b).astype(BF16)
    o_ref[...] += jnp.dot(t, wd_ref[...], preferred_element_type=F32)


def _ffn(x, g, w_gate, w_up, w_down):
    tm, tf = 512, 512
    return pl.pallas_call(
        _ffn_kernel,
        grid=(SEQ // tm, D_FF_DENSE // tf),
        in_specs=[
            pl.BlockSpec((tm, D_MODEL), lambda i, f: (i, 0)),
            pl.BlockSpec((1, D_MODEL), lambda i, f: (0, 0)),
            pl.BlockSpec((D_MODEL, tf), lambda i, f: (0, f)),
            pl.BlockSpec((D_MODEL, tf), lambda i, f: (0, f)),
            pl.BlockSpec((tf, D_MODEL), lambda i, f: (f, 0)),
        ],
        out_specs=pl.BlockSpec((tm, D_MODEL), lambda i, f: (i, 0)),
        out_shape=jax.ShapeDtypeStruct((SEQ, D_MODEL), F32),
        scratch_shapes=[pltpu.VMEM((tm, D_MODEL), BF16)],
        compiler_params=_params("parallel", "arbitrary"),
        name="ffn",
    )(x, g, w_gate, w_up, w_down)


def _moe_kernel(x_ref, g_ref, wr_ref, wg_ref, wu_ref, wd_ref, gf_ref, o_ref, h_ref, rw_ref, acc_ref):
    e = pl.program_id(1)
    f = pl.program_id(2)
    first = jnp.logical_and(e == 0, f == 0)
    last = jnp.logical_and(e == pl.num_programs(1) - 1, f == pl.num_programs(2) - 1)
    tm = x_ref.shape[0]
    lane = lax.broadcasted_iota(jnp.int32, (tm, LANES), 1)

    @pl.when(first)
    def _():
        x = x_ref[...]
        hf = _rms_rows(x, g_ref[...])
        h_ref[...] = hf.astype(BF16)
        acc_ref[...] = x
        logits = jnp.dot(hf, wr_ref[...], preferred_element_type=F32, precision=lax.Precision.HIGHEST)
        lg = jnp.where(lane < N_EXPERTS, logits, -jnp.inf)
        m1 = jnp.max(lg, axis=-1, keepdims=True)
        i1 = jnp.min(jnp.where(lg == m1, lane, LANES), axis=-1, keepdims=True)
        lg2 = jnp.where(lane == i1, -jnp.inf, lg)
        m2 = jnp.max(lg2, axis=-1, keepdims=True)
        i2 = jnp.min(jnp.where(lg2 == m2, lane, LANES), axis=-1, keepdims=True)
        e2 = jnp.exp(m2 - m1)
        g1 = 1.0 / (1.0 + e2)
        g2 = e2 / (1.0 + e2)
        rw_ref[...] = jnp.where(lane == i1, g1, 0.0) + jnp.where(lane == i2, g2, 0.0)

    h = h_ref[...]
    we = jnp.sum(jnp.where(lane == e, rw_ref[...], 0.0), axis=-1, keepdims=True)
    a = jnp.dot(h, wg_ref[...], preferred_element_type=F32)
    b = jnp.dot(h, wu_ref[...], preferred_element_type=F32)
    t = (a * _sigmoid(a) * b * we).astype(BF16)
    acc_ref[...] += jnp.dot(t, wd_ref[...], preferred_element_type=F32)

    @pl.when(last)
    def _():
        o_ref[...] = _rms_rows(acc_ref[...], gf_ref[...])


def _moe(x, g, w_router, w_gate, w_up, w_down, g_final):
    tm, tf = 512, 512
    return pl.pallas_call(
        _moe_kernel,
        grid=(SEQ // tm, N_EXPERTS, D_FF_EXPERT // tf),
        in_specs=[
            pl.BlockSpec((tm, D_MODEL), lambda i, e, f: (i, 0)),
            pl.BlockSpec((1, D_MODEL), lambda i, e, f: (0, 0)),
            pl.BlockSpec((D_MODEL, LANES), lambda i, e, f: (0, 0)),
            pl.BlockSpec((None, D_MODEL, tf), lambda i, e, f: (e, 0, f)),
            pl.BlockSpec((None, D_MODEL, tf), lambda i, e, f: (e, 0, f)),
            pl.BlockSpec((None, tf, D_MODEL), lambda i, e, f: (e, f, 0)),
            pl.BlockSpec((1, D_MODEL), lambda i, e, f: (0, 0)),
        ],
        out_specs=pl.BlockSpec((tm, D_MODEL), lambda i, e, f: (i, 0)),
        out_shape=jax.ShapeDtypeStruct((SEQ, D_MODEL), F32),
        scratch_shapes=[
            pltpu.VMEM((tm, D_MODEL), BF16),
            pltpu.VMEM((tm, LANES), F32),
            pltpu.VMEM((tm, D_MODEL), F32),
        ],
        compiler_params=_params("parallel", "arbitrary", "arbitrary"),
        name="moe",
    )(x, g, w_router, w_gate, w_up, w_down, g_final)


def kernel(x, mix_norm, ffn_norm, final_norm, w_in, ssm_a_re, ssm_a_im, ssm_log_dt, ssm_b_re, ssm_b_im,
           ssm_c_re, ssm_c_im, ssm_d, w_glu, p_attn, p_ssm, w_out, ffn_w_gate, ffn_w_up, ffn_w_down,
           w_router, moe_w_gate, moe_w_up, moe_w_down):
    assert x.shape == (1, SEQ, D_MODEL) and DEPTH == 2
    xs = x.reshape(SEQ, D_MODEL)
    col_scale = jnp.concatenate([jnp.full((SB_WIDTH,), HEAD_DIM ** -0.5 * LOG2E, F32),
                                 jnp.ones((IN_COLS - SB_WIDTH,), F32)])
    for layer in range(DEPTH):
        w_in_b = (w_in[layer] * col_scale).astype(BF16)
        proj, u, vt = _inproj(xs, mix_norm[layer][None], w_in_b)
        o_attn = _attention(proj, vt)
        ssm_w = _ssm_weights(ssm_a_re[layer], ssm_a_im[layer], ssm_log_dt[layer], ssm_b_re[layer],
                             ssm_b_im[layer], ssm_c_re[layer], ssm_c_im[layer], ssm_d[layer])
        y = _ssm(u, *ssm_w)
        xs = _post(xs, o_attn, y, proj, w_glu[layer].astype(BF16), p_attn[layer].astype(BF16),
                   p_ssm[layer].astype(BF16), w_out[layer].astype(BF16))
        i = layer // 2
        if layer % 2 == 0:
            xs = _ffn(xs, ffn_norm[layer][None], ffn_w_gate[i].astype(BF16), ffn_w_up[i].astype(BF16),
                      ffn_w_down[i].astype(BF16))
        else:
            wr = jnp.pad(w_router[i], ((0, 0), (0, LANES - N_EXPERTS)))
            xs = _moe(xs, ffn_norm[layer][None], wr, moe_w_gate[i].astype(BF16), moe_w_up[i].astype(BF16),
                      moe_w_down[i].astype(BF16), final_norm[None])
    return xs.reshape(1, SEQ, D_MODEL)
```

```python
import functools

import jax
import jax.numpy as jnp
from jax import lax
from jax.experimental import pallas as pl
from jax.experimental.pallas import tpu as pltpu

F32 = jnp.float32
BF16 = jnp.bfloat16

D_MODEL = 2048
SEQ = 16384
DEPTH = 2
HEAD_DIM = 128
HEADS = 8
SB_WIDTH = HEADS * HEAD_DIM
SSM_WIDTH = 1024
SSM_GROUP = 16
SSM_GROUPS = 64
SSM_STATE = 64
IN_COLS = 3 * SB_WIDTH + SSM_WIDTH + 2 * D_MODEL
D_FF_DENSE = 2 * D_MODEL
N_EXPERTS = 8
D_FF_EXPERT = D_MODEL // 2
RMS_EPS = 1e-6

LANES = 128
SUBLANES = 8
VMEM_LIMIT = 56 * 1024 * 1024

TQ = 256
TK = 256
HP = 4
LOG2E = 1.4426950408889634
UNDERFLOW_LOG2 = 152.0
NORM_MARGIN = 1.01
CHUNK = 8
N_CHUNKS = SEQ // CHUNK
SLAB = LANES
N_SLABS = SSM_WIDTH // SLAB
SLAB_GROUPS = SLAB // SSM_GROUP
SLAB_STATE = SLAB_GROUPS * SSM_STATE
SSM_ROWS = 256


def _params(*sem):
    return pltpu.CompilerParams(dimension_semantics=sem, vmem_limit_bytes=VMEM_LIMIT)


def _rms_rows(x, g):
    ms = jnp.mean(x * x, axis=-1, keepdims=True)
    return x * lax.rsqrt(ms + RMS_EPS) * g


def _sigmoid(x):
    return 1.0 / (1.0 + jnp.exp(-x))


def _inproj_kernel(x_ref, g_ref, w_ref, p_ref, u_ref, vt_ref, h_ref, s_ref):
    j = pl.program_id(1)

    @pl.when(j == 0)
    def _():
        h_ref[...] = _rms_rows(x_ref[...], g_ref[...]).astype(BF16)

    r = jnp.dot(h_ref[...], w_ref[...], preferred_element_type=F32)
    p_ref[...] = r.astype(BF16)

    @pl.when(j == 2)
    def _():
        rt = r.T
        for h in range(HEADS):
            for b in range(vt_ref.shape[1]):
                vt_ref[h, b] = rt[h * HEAD_DIM:(h + 1) * HEAD_DIM, b * TK:(b + 1) * TK].astype(BF16)

    @pl.when(j == 3)
    def _():
        slabs = s_ref.shape[0]
        rows = s_ref.shape[1] // CHUNK
        for l in range(slabs):
            s_ref[l] = r[:, l * LANES:(l + 1) * LANES]
        u_ref[...] = jnp.concatenate(
            [s_ref[l, pl.ds(c, rows, stride=CHUNK), :] for c in range(CHUNK) for l in range(slabs)],
            axis=1).astype(BF16)


def _inproj(x, g, w):
    tm, tn = 1024, 1024
    assert tn == SB_WIDTH == SSM_WIDTH and tm % TK == 0 and tm % CHUNK == 0
    return pl.pallas_call(
        _inproj_kernel,
        grid=(SEQ // tm, IN_COLS // tn),
        in_specs=[
            pl.BlockSpec((tm, D_MODEL), lambda i, j: (i, 0), pipeline_mode=pl.Buffered(1)),
            pl.BlockSpec((1, D_MODEL), lambda i, j: (0, 0)),
            pl.BlockSpec((D_MODEL, tn), lambda i, j: (0, j)),
        ],
        out_specs=[
            pl.BlockSpec((tm, tn), lambda i, j: (i, j)),
            pl.BlockSpec((tm // CHUNK, CHUNK * SSM_WIDTH), lambda i, j: (i, 0)),
            pl.BlockSpec((HEADS, tm // TK, HEAD_DIM, TK), lambda i, j: (0, i, 0, 0)),
        ],
        out_shape=[
            jax.ShapeDtypeStruct((SEQ, IN_COLS), BF16),
            jax.ShapeDtypeStruct((N_CHUNKS, CHUNK * SSM_WIDTH), BF16),
            jax.ShapeDtypeStruct((HEADS, SEQ // TK, HEAD_DIM, TK), BF16),
        ],
        scratch_shapes=[pltpu.VMEM((tm, D_MODEL), BF16), pltpu.VMEM((SSM_WIDTH // LANES, tm, LANES), F32)],
        compiler_params=_params("parallel", "arbitrary"),
        name="inproj",
    )(x, g, w)


def _attn_kernel(q_ref, k_ref, vt_ref, o_ref, kmax_ref, carry_ref, acc_ref):
    i = pl.program_id(1)
    hs = lambda h: slice(h * HEAD_DIM, (h + 1) * HEAD_DIM)

    @pl.when(i == 0)
    def _():
        def body(j, run):
            ks = pl.multiple_of(j * TK, TK)
            new, rows = [], []
            for h in range(HP):
                kb = k_ref[pl.ds(ks, TK), hs(h)].astype(F32)
                n2 = jnp.max(jnp.sum(kb * kb, axis=1, keepdims=True), axis=0, keepdims=True)
                new.append(jnp.maximum(run[h], n2))
                rows.append(jnp.broadcast_to(jnp.sqrt(new[h]), (1, TQ)))
            kmax_ref[j] = jnp.concatenate(rows, axis=0)
            return tuple(new)

        lax.fori_loop(0, SEQ // TK, body, tuple(jnp.zeros((1, 1), F32) for _ in range(HP)))

    row = lax.broadcasted_iota(jnp.int32, (TK, TQ), 0)
    col = lax.broadcasted_iota(jnp.int32, (TK, TQ), 1)
    upper = jnp.where(col >= row, 1.0, 0.0).astype(BF16)
    causal = row < col

    def block(h, j, carry, acc, masked):
        ks = pl.multiple_of(j * TK, TK)
        kb = k_ref[pl.ds(ks, TK), hs(h)]
        zt = lax.dot_general(kb, q_ref[:, hs(h)], (((1,), (1,)), ((), ())),
                             preferred_element_type=F32)
        neg_abs = lax.bitcast_convert_type(
            lax.bitcast_convert_type(zt, jnp.uint32) | jnp.uint32(0x80000000), F32)
        sp = jnp.maximum(zt, 0.0) + jnp.log(1.0 + jnp.exp2(neg_abs)) * LOG2E
        if masked:
            sp = jnp.where(causal, sp, 0.0)
        rt = jnp.dot(upper, sp.astype(BF16), preferred_element_type=F32)
        w = jnp.exp2(zt - rt - carry)
        if masked:
            w = jnp.where(causal, w, 0.0)
        acc = acc + jnp.dot(vt_ref[h, j], w.astype(BF16), preferred_element_type=F32)
        return carry + rt[0:1, :], acc

    prev = jnp.maximum(i - 1, 0)
    no_prev = jnp.where(i == 0, 1e30, 0.0).astype(F32)
    qmax = []
    for h in range(HP):
        carry, acc = block(h, i, jnp.zeros((1, TQ), F32), jnp.zeros((HEAD_DIM, TQ), F32), True)
        carry, acc = block(h, prev, carry + no_prev, acc, False)
        carry_ref[h] = carry
        acc_ref[h] = acc
        qf = q_ref[:, hs(h)].astype(F32)
        n2 = jnp.max(jnp.sum(qf * qf, axis=1, keepdims=True), axis=0, keepdims=True)
        qmax.append(jnp.sqrt(n2) * NORM_MARGIN)

    def more_needed(j):
        km = kmax_ref[j]
        worst = None
        for h in range(HP):
            slack = carry_ref[h] - (UNDERFLOW_LOG2 + qmax[h] * km[h:h + 1, :])
            worst = slack if worst is None else jnp.minimum(worst, slack)
        return (jnp.min(worst) < 0.0).astype(jnp.int32)

    def cond(c):
        return jnp.logical_and(c[0] >= 0, c[1] > 0)

    def body(c):
        j = c[0]
        for h in range(HP):
            carry, acc = block(h, j, carry_ref[h], acc_ref[h], False)
            carry_ref[h] = carry
            acc_ref[h] = acc
        return j - 1, more_needed(jnp.maximum(j - 1, 0))

    lax.while_loop(cond, body, (i - 2, more_needed(jnp.maximum(i - 2, 0))))
    for h in range(HP):
        o_ref[:, hs(h)] = acc_ref[h].T.astype(o_ref.dtype)


def _attention(proj, vt):
    wide = HP * HEAD_DIM
    once = pl.Buffered(1)
    return pl.pallas_call(
        _attn_kernel,
        grid=(HEADS // HP, SEQ // TQ),
        in_specs=[
            pl.BlockSpec((TQ, wide), lambda g, i: (i, g)),
            pl.BlockSpec((SEQ, wide), lambda g, i: (0, HEADS // HP + g), pipeline_mode=once),
            pl.BlockSpec((HP, SEQ // TK, HEAD_DIM, TK), lambda g, i: (g, 0, 0, 0), pipeline_mode=once),
        ],
        out_specs=pl.BlockSpec((TQ, wide), lambda g, i: (i, g)),
        out_shape=jax.ShapeDtypeStruct((SEQ, SB_WIDTH), BF16),
        scratch_shapes=[
            pltpu.VMEM((SEQ // TK, HP, TQ), F32),
            pltpu.VMEM((HP, 1, TQ), F32),
            pltpu.VMEM((HP, HEAD_DIM, TQ), F32),
        ],
        compiler_params=_params("arbitrary", "arbitrary"),
        name="attention",
    )(proj, proj, vt)


def _ssm_weights(a_re, a_im, log_dt, b_re, b_im, c_re, c_im, d_skip):
    dt = jnp.exp(log_dt)[:, None]
    d = jnp.arange(CHUNK + 1, dtype=F32)[:, None, None]
    mag = jnp.exp(d * (dt * a_re))
    pw_re = mag * jnp.cos(d * (dt * a_im))
    pw_im = mag * jnp.sin(d * (dt * a_im))
    abar_re, abar_im = pw_re[1], pw_im[1]
    den = a_re * a_re + a_im * a_im
    num_re = abar_re - 1.0
    zoh_re = (num_re * a_re + abar_im * a_im) / den
    zoh_im = (abar_im * a_re - num_re * a_im) / den
    bb_re = zoh_re[..., None] * b_re - zoh_im[..., None] * b_im
    bb_im = zoh_re[..., None] * b_im + zoh_im[..., None] * b_re
    ca_re = c_re[None] * pw_re[:, :, None, :] - c_im[None] * pw_im[:, :, None, :]
    ca_im = c_re[None] * pw_im[:, :, None, :] + c_im[None] * pw_re[:, :, None, :]
    kd = jnp.einsum('dgon,gnp->dgop', ca_re, bb_re) - jnp.einsum('dgon,gnp->dgop', ca_im, bb_im)
    kd = kd.at[0].add(jnp.eye(SSM_GROUP, dtype=F32)[None] * d_skip[:, :, None])
    lag = jnp.arange(CHUNK)[None, :] - jnp.arange(CHUNK)[:, None]
    tg = jnp.where((lag >= 0)[:, :, None, None, None], kd[jnp.clip(lag, 0, CHUNK)], 0.0)
    tg = tg.transpose(2, 0, 4, 1, 3)

    def own(width):
        return jnp.repeat(jnp.eye(SLAB_GROUPS, dtype=F32), width, axis=1)[None, None, :, None, None, :]

    tg = tg.reshape(N_SLABS, SLAB_GROUPS, CHUNK, SSM_GROUP, CHUNK, SSM_GROUP).transpose(0, 2, 3, 4, 1, 5)
    tg = tg.reshape(N_SLABS, CHUNK, 1, SSM_GROUP, CHUNK, SLAB)
    t_mat = (tg * own(SSM_GROUP)).reshape(N_SLABS, CHUNK * SLAB, CHUNK * SLAB)
    rev_re, rev_im = pw_re[CHUNK - 1::-1], pw_im[CHUNK - 1::-1]
    wz_re = rev_re[:, :, :, None] * bb_re[None] - rev_im[:, :, :, None] * bb_im[None]
    wz_im = rev_re[:, :, :, None] * bb_im[None] + rev_im[:, :, :, None] * bb_re[None]
    wz = jnp.stack([wz_re, wz_im], axis=0).transpose(2, 1, 4, 0, 3)
    wz = wz.reshape(N_SLABS, SLAB_GROUPS, CHUNK, SSM_GROUP, 2, SSM_STATE).transpose(0, 2, 3, 4, 1, 5)
    wz = wz.reshape(N_SLABS, CHUNK, 1, SSM_GROUP, 2, SLAB_STATE)
    wz_mat = (wz * own(SSM_STATE)).reshape(N_SLABS, CHUNK * SLAB, 2 * SLAB_STATE)
    v = jnp.stack([ca_re[1:], -ca_im[1:]], axis=0).transpose(2, 0, 4, 1, 3)
    v = v.reshape(N_SLABS, SLAB_GROUPS, 2, SSM_STATE, CHUNK, SSM_GROUP).transpose(0, 2, 3, 4, 1, 5)
    v = v.reshape(N_SLABS, 2, 1, SSM_STATE, CHUNK, SLAB)
    v_mat = (v * own(SSM_GROUP)).reshape(N_SLABS, 2 * SLAB_STATE, CHUNK * SLAB)
    lam = jnp.stack([pw_re[CHUNK], pw_im[CHUNK]], axis=1)
    lam = lam.reshape(N_SLABS, SLAB_GROUPS, 2, SSM_STATE).transpose(0, 2, 1, 3)
    lam = lam.reshape(N_SLABS, 2, SLAB_STATE)
    return wz_mat.astype(BF16), t_mat.astype(BF16), v_mat.astype(BF16), lam


def _cmul(ar, ai, br, bi):
    return ar * br - ai * bi, ar * bi + ai * br


def _ssm_kernel(*refs):
    u_refs = refs[:CHUNK]
    wz_ref, t_ref, v_ref, lam_ref, o_ref, s_ref = refs[CHUNK:]
    ns = SLAB_STATE
    lam1 = (lam_ref[0:1, :], lam_ref[1:2, :])
    lam2 = _cmul(*lam1, *lam1)
    lam4 = _cmul(*lam2, *lam2)
    lam8 = _cmul(*lam4, *lam4)

    def lhs_rows(r0, n):
        return jnp.concatenate([u[pl.ds(r0, n), :] for u in u_refs], axis=1)

    def pass1(b, _):
        r0 = pl.multiple_of(b * SSM_ROWS, SSM_ROWS)
        z = jnp.dot(lhs_rows(r0, SSM_ROWS), wz_ref[...], preferred_element_type=F32)
        zr, zi = z[:, :ns], z[:, ns:]
        sub = lax.broadcasted_iota(jnp.int32, (SSM_ROWS, ns), 0) & (SUBLANES - 1)
        for shift, (cr, ci) in ((1, lam1), (2, lam2), (4, lam4)):
            keep = sub >= shift
            pr = jnp.where(keep, pltpu.roll(zr, shift, axis=0), 0.0)
            pi = jnp.where(keep, pltpu.roll(zi, shift, axis=0), 0.0)
            ar, ai = _cmul(cr, ci, pr, pi)
            zr, zi = zr + ar, zi + ai
        s_ref[pl.ds(r0, SSM_ROWS), :ns] = zr
        s_ref[pl.ds(r0, SSM_ROWS), ns:] = zi
        return 0

    lax.fori_loop(0, N_CHUNKS // SSM_ROWS, pass1, 0)

    pow_re, pow_im = [jnp.ones((1, ns), F32)], [jnp.zeros((1, ns), F32)]
    for _ in range(SUBLANES - 1):
        nr, ni = _cmul(pow_re[-1], pow_im[-1], *lam1)
        pow_re.append(nr)
        pow_im.append(ni)
    tab_re = jnp.concatenate(pow_re, axis=0)
    tab_im = jnp.concatenate(pow_im, axis=0)
    sub8 = lax.broadcasted_iota(jnp.int32, (SUBLANES, ns), 0)

    def pass2(t, c):
        cr, ci = c
        r0 = pl.multiple_of(t * SUBLANES, SUBLANES)
        lr = s_ref[pl.ds(r0, SUBLANES), :ns]
        li = s_ref[pl.ds(r0, SUBLANES), ns:]
        er = jnp.where(sub8 >= 1, pltpu.roll(lr, 1, axis=0), 0.0)
        ei = jnp.where(sub8 >= 1, pltpu.roll(li, 1, axis=0), 0.0)
        ar, ai = _cmul(tab_re, tab_im, cr, ci)
        s_ref[pl.ds(r0, SUBLANES), :ns] = er + ar
        s_ref[pl.ds(r0, SUBLANES), ns:] = ei + ai
        nr, ni = _cmul(*lam8, cr, ci)
        return lr[SUBLANES - 1:SUBLANES, :] + nr, li[SUBLANES - 1:SUBLANES, :] + ni

    zero = jnp.zeros((1, ns), F32)
    lax.fori_loop(0, N_CHUNKS // SUBLANES, pass2, (zero, zero))

    for b in range(N_CHUNKS // SSM_ROWS):
        r0 = b * SSM_ROWS
        y = jnp.dot(lhs_rows(r0, SSM_ROWS), t_ref[...], preferred_element_type=F32)
        y = y + jnp.dot(s_ref[pl.ds(r0, SSM_ROWS), :].astype(BF16), v_ref[...],
                        preferred_element_type=F32)
        for j in range(CHUNK):
            o_ref[pl.ds(r0 * CHUNK + j, SSM_ROWS, stride=CHUNK), :] = y[:, j * SLAB:(j + 1) * SLAB]


def _ssm(u2, wz_mat, t_mat, v_mat, lam):
    u_specs = [pl.BlockSpec((N_CHUNKS, SLAB), functools.partial(lambda j, k: (0, j * N_SLABS + k), j))
               for j in range(CHUNK)]
    return pl.pallas_call(
        _ssm_kernel,
        grid=(N_SLABS,),
        in_specs=u_specs + [
            pl.BlockSpec((None, CHUNK * SLAB, 2 * SLAB_STATE), lambda k: (k, 0, 0)),
            pl.BlockSpec((None, CHUNK * SLAB, CHUNK * SLAB), lambda k: (k, 0, 0)),
            pl.BlockSpec((None, 2 * SLAB_STATE, CHUNK * SLAB), lambda k: (k, 0, 0)),
            pl.BlockSpec((None, 2, SLAB_STATE), lambda k: (k, 0, 0)),
        ],
        out_specs=pl.BlockSpec((SEQ, SLAB), lambda k: (0, k), pipeline_mode=pl.Buffered(1)),
        out_shape=jax.ShapeDtypeStruct((SEQ, SSM_WIDTH), F32),
        scratch_shapes=[pltpu.VMEM((N_CHUNKS, 2 * SLAB_STATE), F32)],
        compiler_params=_params("arbitrary"),
        name="ssm",
    )(*([u2] * CHUNK), wz_mat, t_mat, v_mat, lam)


def _gelu_tanh(x):
    c = 0.7978845608028654
    return 0.5 * x * (1.0 + jnp.tanh(c * (x + 0.044715 * (x * x * x))))


def _post_kernel(x_ref, oa_ref, y_ref, ga_ref, gb_ref, wglu_ref, pa_ref, ps_ref, wo_ref, o_ref):
    y = _gelu_tanh(y_ref[...].astype(F32))
    yb = y.astype(BF16)
    glu = jnp.dot(yb, wglu_ref[...], preferred_element_type=F32)
    o_ssm = (y * _sigmoid(glu)).astype(BF16)
    a = jnp.dot(oa_ref[...], pa_ref[...], preferred_element_type=F32)
    b = jnp.dot(o_ssm, ps_ref[...], preferred_element_type=F32)
    merged = _sigmoid(ga_ref[...].astype(F32)) * a + _sigmoid(gb_ref[...].astype(F32)) * b
    o_ref[...] = x_ref[...] + jnp.dot(merged.astype(BF16), wo_ref[...], preferred_element_type=F32)


def _post(x, o_attn, y, proj, w_glu, p_attn, p_ssm, w_out):
    tm = 256
    ga_blk = (3 * SB_WIDTH + SSM_WIDTH) // D_MODEL
    const = lambda shape: pl.BlockSpec(shape, lambda i: (0, 0), pipeline_mode=pl.Buffered(1))
    return pl.pallas_call(
        _post_kernel,
        grid=(SEQ // tm,),
        in_specs=[
            pl.BlockSpec((tm, D_MODEL), lambda i: (i, 0)),
            pl.BlockSpec((tm, SB_WIDTH), lambda i: (i, 0)),
            pl.BlockSpec((tm, SSM_WIDTH), lambda i: (i, 0)),
            pl.BlockSpec((tm, D_MODEL), lambda i: (i, ga_blk)),
            pl.BlockSpec((tm, D_MODEL), lambda i: (i, ga_blk + 1)),
            const((SSM_WIDTH, SSM_WIDTH)),
            const((SB_WIDTH, D_MODEL)),
            const((SSM_WIDTH, D_MODEL)),
            const((D_MODEL, D_MODEL)),
        ],
        out_specs=pl.BlockSpec((tm, D_MODEL), lambda i: (i, 0)),
        out_shape=jax.ShapeDtypeStruct((SEQ, D_MODEL), F32),
        compiler_params=_params("parallel"),
        name="post",
    )(x, o_attn, y, proj, proj, w_glu, p_attn, p_ssm, w_out)


def _ffn_kernel(x_ref, g_ref, wg_ref, wu_ref, wd_ref, o_ref, h_ref):
    f = pl.program_id(1)

    @pl.when(f == 0)
    def _():
        x = x_ref[...]
        h_ref[...] = _rms_rows(x, g_ref[...]).astype(BF16)
        o_ref[...] = x

    h = h_ref[...]
    a = jnp.dot(h, wg_ref[...], preferred_element_type=F32)
    b = jnp.dot(h, wu_ref[...], preferred_element_type=F32)
    t = (a * _sigmoid(a) * b).astype(BF16)
    o_ref[...] += jnp.dot(t, wd_ref[...], preferred_element_type=F32)


def _ffn(x, g, w_gate, w_up, w_down):
    tm, tf = 1024, 256
    return pl.pallas_call(
        _ffn_kernel,
        grid=(SEQ // tm, D_FF_DENSE // tf),
        in_specs=[
            pl.BlockSpec((tm, D_MODEL), lambda i, f: (i, 0), pipeline_mode=pl.Buffered(1)),
            pl.BlockSpec((1, D_MODEL), lambda i, f: (0, 0)),
            pl.BlockSpec((D_MODEL, tf), lambda i, f: (0, f)),
            pl.BlockSpec((D_MODEL, tf), lambda i, f: (0, f)),
            pl.BlockSpec((tf, D_MODEL), lambda i, f: (f, 0)),
        ],
        out_specs=pl.BlockSpec((tm, D_MODEL), lambda i, f: (i, 0)),
        out_shape=jax.ShapeDtypeStruct((SEQ, D_MODEL), F32),
        scratch_shapes=[pltpu.VMEM((tm, D_MODEL), BF16)],
        compiler_params=_params("parallel", "arbitrary"),
        name="ffn",
    )(x, g, w_gate, w_up, w_down)


def _moe_kernel(x_ref, g_ref, wr_ref, wg_ref, wu_ref, wd_ref, gf_ref, o_ref, h_ref, rw_ref, acc_ref):
    e = pl.program_id(1)
    f = pl.program_id(2)
    first = jnp.logical_and(e == 0, f == 0)
    last = jnp.logical_and(e == pl.num_programs(1) - 1, f == pl.num_programs(2) - 1)
    tm = x_ref.shape[0]
    lane = lax.broadcasted_iota(jnp.int32, (tm, LANES), 1)

    @pl.when(first)
    def _():
        x = x_ref[...]
        hf = _rms_rows(x, g_ref[...])
        h_ref[...] = hf.astype(BF16)
        acc_ref[...] = x
        logits = jnp.dot(hf, wr_ref[...], preferred_element_type=F32, precision=lax.Precision.HIGHEST)
        lg = jnp.where(lane < N_EXPERTS, logits, -jnp.inf)
        m1 = jnp.max(lg, axis=-1, keepdims=True)
        i1 = jnp.min(jnp.where(lg == m1, lane, LANES), axis=-1, keepdims=True)
        lg2 = jnp.where(lane == i1, -jnp.inf, lg)
        m2 = jnp.max(lg2, axis=-1, keepdims=True)
        i2 = jnp.min(jnp.where(lg2 == m2, lane, LANES), axis=-1, keepdims=True)
        e2 = jnp.exp(m2 - m1)
        g1 = 1.0 / (1.0 + e2)
        g2 = e2 / (1.0 + e2)
        rw_ref[...] = jnp.where(lane == i1, g1, 0.0) + jnp.where(lane == i2, g2, 0.0)

    h = h_ref[...]
    we = jnp.sum(jnp.where(lane == e, rw_ref[...], 0.0), axis=-1, keepdims=True)
    a = jnp.dot(h, wg_ref[...], preferred_element_type=F32)
    b = jnp.dot(h, wu_ref[...], preferred_element_type=F32)
    t = (a * _sigmoid(a) * b * we).astype(BF16)
    acc_ref[...] += jnp.dot(t, wd_ref[...], preferred_element_type=F32)

    @pl.when(last)
    def _():
        o_ref[...] = _rms_rows(acc_ref[...], gf_ref[...])


def _moe(x, g, w_router, w_gate, w_up, w_down, g_final):
    tm, tf = 1024, 256
    return pl.pallas_call(
        _moe_kernel,
        grid=(SEQ // tm, N_EXPERTS, D_FF_EXPERT // tf),
        in_specs=[
            pl.BlockSpec((tm, D_MODEL), lambda i, e, f: (i, 0), pipeline_mode=pl.Buffered(1)),
            pl.BlockSpec((1, D_MODEL), lambda i, e, f: (0, 0)),
            pl.BlockSpec((D_MODEL, LANES), lambda i, e, f: (0, 0)),
            pl.BlockSpec((None, D_MODEL, tf), lambda i, e, f: (e, 0, f)),
            pl.BlockSpec((None, D_MODEL, tf), lambda i, e, f: (e, 0, f)),
            pl.BlockSpec((None, tf, D_MODEL), lambda i, e, f: (e, f, 0)),
            pl.BlockSpec((1, D_MODEL), lambda i, e, f: (0, 0)),
        ],
        out_specs=pl.BlockSpec((tm, D_MODEL), lambda i, e, f: (i, 0), pipeline_mode=pl.Buffered(1)),
        out_shape=jax.ShapeDtypeStruct((SEQ, D_MODEL), F32),
        scratch_shapes=[
            pltpu.VMEM((tm, D_MODEL), BF16),
            pltpu.VMEM((tm, LANES), F32),
            pltpu.VMEM((tm, D_MODEL), F32),
        ],
        compiler_params=_params("parallel", "arbitrary", "arbitrary"),
        name="moe",
    )(x, g, w_router, w_gate, w_up, w_down, g_final)


def kernel(x, mix_norm, ffn_norm, final_norm, w_in, ssm_a_re, ssm_a_im, ssm_log_dt, ssm_b_re, ssm_b_im,
           ssm_c_re, ssm_c_im, ssm_d, w_glu, p_attn, p_ssm, w_out, ffn_w_gate, ffn_w_up, ffn_w_down,
           w_router, moe_w_gate, moe_w_up, moe_w_down):
    assert x.shape == (1, SEQ, D_MODEL) and DEPTH == 2
    xs = x.reshape(SEQ, D_MODEL)
    col_scale = jnp.concatenate([jnp.full((SB_WIDTH,), HEAD_DIM ** -0.5 * LOG2E, F32),
                                 jnp.ones((IN_COLS - SB_WIDTH,), F32)])
    for layer in range(DEPTH):
        w_in_b = (w_in[layer] * col_scale).astype(BF16)
        proj, u, vt = _inproj(xs, mix_norm[layer][None], w_in_b)
        o_attn = _attention(proj, vt)
        ssm_w = _ssm_weights(ssm_a_re[layer], ssm_a_im[layer], ssm_log_dt[layer], ssm_b_re[layer],
                             ssm_b_im[layer], ssm_c_re[layer], ssm_c_im[layer], ssm_d[layer])
        y = _ssm(u, *ssm_w)
        xs = _post(xs, o_attn, y, proj, w_glu[layer].astype(BF16), p_attn[layer].astype(BF16),
                   p_ssm[layer].astype(BF16), w_out[layer].astype(BF16))
        i = layer // 2
        if layer % 2 == 0:
            xs = _ffn(xs, ffn_norm[layer][None], ffn_w_gate[i].astype(BF16), ffn_w_up[i].astype(BF16),
                      ffn_w_down[i].astype(BF16))
        else:
            wr = jnp.pad(w_router[i], ((0, 0), (0, LANES - N_EXPERTS)))
            xs = _moe(xs, ffn_norm[layer][None], wr, moe_w_gate[i].astype(BF16), moe_w_up[i].astype(BF16),
                      moe_w_down[i].astype(BF16), final_norm[None])
    return xs.reshape(1, SEQ, D_MODEL)
```

```python
import functools

import jax
import jax.numpy as jnp
from jax import lax
from jax.experimental import pallas as pl
from jax.experimental.pallas import tpu as pltpu

F32 = jnp.float32
BF16 = jnp.bfloat16

D_MODEL = 2048
SEQ = 16384
DEPTH = 2
HEAD_DIM = 128
HEADS = 8
SB_WIDTH = HEADS * HEAD_DIM
SSM_WIDTH = 1024
SSM_GROUP = 16
SSM_GROUPS = 64
SSM_STATE = 64
IN_COLS = 3 * SB_WIDTH + SSM_WIDTH + 2 * D_MODEL
D_FF_DENSE = 2 * D_MODEL
N_EXPERTS = 8
D_FF_EXPERT = D_MODEL // 2
RMS_EPS = 1e-6

LANES = 128
SUBLANES = 8
VMEM_LIMIT = 56 * 1024 * 1024

TQ = 256
TK = 256
HP = 4
LOG2E = 1.4426950408889634
UNDERFLOW_LOG2 = 152.0
NORM_MARGIN = 1.01
CHUNK = 8
N_CHUNKS = SEQ // CHUNK
SLAB = LANES
N_SLABS = SSM_WIDTH // SLAB
SLAB_GROUPS = SLAB // SSM_GROUP
SLAB_STATE = SLAB_GROUPS * SSM_STATE
SSM_ROWS = 256


def _params(*sem):
    return pltpu.CompilerParams(dimension_semantics=sem, vmem_limit_bytes=VMEM_LIMIT)


def _rms_rows(x, g):
    ms = jnp.mean(x * x, axis=-1, keepdims=True)
    return x * lax.rsqrt(ms + RMS_EPS) * g


def _sigmoid(x):
    return 1.0 / (1.0 + jnp.exp(-x))


def _inproj_kernel(x_ref, g_ref, w_ref, p_ref, u_ref, vt_ref, h_ref, s_ref):
    j = pl.program_id(1)

    @pl.when(j == 0)
    def _():
        h_ref[...] = _rms_rows(x_ref[...], g_ref[...]).astype(BF16)

    r = jnp.dot(h_ref[...], w_ref[...], preferred_element_type=F32)
    p_ref[...] = r.astype(BF16)

    @pl.when(j == 2)
    def _():
        rt = r.T
        for h in range(HEADS):
            for b in range(vt_ref.shape[1]):
                vt_ref[h, b] = rt[h * HEAD_DIM:(h + 1) * HEAD_DIM, b * TK:(b + 1) * TK].astype(BF16)

    @pl.when(j == 3)
    def _():
        slabs = s_ref.shape[0]
        rows = s_ref.shape[1] // CHUNK
        for l in range(slabs):
            s_ref[l] = r[:, l * LANES:(l + 1) * LANES]
        u_ref[...] = jnp.concatenate(
            [s_ref[l, pl.ds(c, rows, stride=CHUNK), :] for c in range(CHUNK) for l in range(slabs)],
            axis=1).astype(BF16)


def _inproj(x, g, w):
    tm, tn = 512, 1024
    assert tn == SB_WIDTH == SSM_WIDTH and tm % TK == 0 and tm % CHUNK == 0
    return pl.pallas_call(
        _inproj_kernel,
        grid=(SEQ // tm, IN_COLS // tn),
        in_specs=[
            pl.BlockSpec((tm, D_MODEL), lambda i, j: (i, 0)),
            pl.BlockSpec((1, D_MODEL), lambda i, j: (0, 0)),
            pl.BlockSpec((D_MODEL, tn), lambda i, j: (0, j)),
        ],
        out_specs=[
            pl.BlockSpec((tm, tn), lambda i, j: (i, j)),
            pl.BlockSpec((tm // CHUNK, CHUNK * SSM_WIDTH), lambda i, j: (i, 0)),
            pl.BlockSpec((HEADS, tm // TK, HEAD_DIM, TK), lambda i, j: (0, i, 0, 0)),
        ],
        out_shape=[
            jax.ShapeDtypeStruct((SEQ, IN_COLS), BF16),
            jax.ShapeDtypeStruct((N_CHUNKS, CHUNK * SSM_WIDTH), BF16),
            jax.ShapeDtypeStruct((HEADS, SEQ // TK, HEAD_DIM, TK), BF16),
        ],
        scratch_shapes=[pltpu.VMEM((tm, D_MODEL), BF16), pltpu.VMEM((SSM_WIDTH // LANES, tm, LANES), F32)],
        compiler_params=_params("parallel", "arbitrary"),
        name="inproj",
    )(x, g, w)


def _attn_kernel(q_ref, k_ref, vt_ref, o_ref, kmax_ref, carry_ref, acc_ref):
    i = pl.program_id(1)
    hs = lambda h: slice(h * HEAD_DIM, (h + 1) * HEAD_DIM)

    @pl.when(i == 0)
    def _():
        def body(j, run):
            ks = pl.multiple_of(j * TK, TK)
            new, rows = [], []
            for h in range(HP):
                kb = k_ref[pl.ds(ks, TK), hs(h)].astype(F32)
                n2 = jnp.max(jnp.sum(kb * kb, axis=1, keepdims=True), axis=0, keepdims=True)
                new.append(jnp.maximum(run[h], n2))
                rows.append(jnp.broadcast_to(jnp.sqrt(new[h]), (1, TQ)))
            kmax_ref[j] = jnp.concatenate(rows, axis=0)
            return tuple(new)

        lax.fori_loop(0, SEQ // TK, body, tuple(jnp.zeros((1, 1), F32) for _ in range(HP)))

    row = lax.broadcasted_iota(jnp.int32, (TK, TQ), 0)
    col = lax.broadcasted_iota(jnp.int32, (TK, TQ), 1)
    upper = jnp.where(col >= row, 1.0, 0.0).astype(BF16)
    causal = row < col

    def block(h, j, carry, acc, masked):
        ks = pl.multiple_of(j * TK, TK)
        kb = k_ref[pl.ds(ks, TK), hs(h)]
        zt = lax.dot_general(kb, q_ref[:, hs(h)], (((1,), (1,)), ((), ())),
                             preferred_element_type=F32)
        sp = jnp.maximum(zt, 0.0) + jnp.log(1.0 + jnp.exp2(-jnp.abs(zt))) * LOG2E
        if masked:
            sp = jnp.where(causal, sp, 0.0)
        rt = jnp.dot(upper, sp.astype(BF16), preferred_element_type=F32)
        w = jnp.exp2(zt - rt - carry)
        if masked:
            w = jnp.where(causal, w, 0.0)
        acc = acc + jnp.dot(vt_ref[h, j], w.astype(BF16), preferred_element_type=F32)
        return carry + rt[0:1, :], acc

    prev = jnp.maximum(i - 1, 0)
    no_prev = jnp.where(i == 0, 1e30, 0.0).astype(F32)
    qmax = []
    for h in range(HP):
        carry, acc = block(h, i, jnp.zeros((1, TQ), F32), jnp.zeros((HEAD_DIM, TQ), F32), True)
        carry, acc = block(h, prev, carry + no_prev, acc, False)
        carry_ref[h] = carry
        acc_ref[h] = acc
        qf = q_ref[:, hs(h)].astype(F32)
        n2 = jnp.max(jnp.sum(qf * qf, axis=1, keepdims=True), axis=0, keepdims=True)
        qmax.append(jnp.sqrt(n2) * NORM_MARGIN)

    def more_needed(j):
        km = kmax_ref[j]
        worst = None
        for h in range(HP):
            slack = carry_ref[h] - (UNDERFLOW_LOG2 + qmax[h] * km[h:h + 1, :])
            worst = slack if worst is None else jnp.minimum(worst, slack)
        return (jnp.min(worst) < 0.0).astype(jnp.int32)

    def cond(c):
        return jnp.logical_and(c[0] >= 0, c[1] > 0)

    def body(c):
        j = c[0]
        for h in range(HP):
            carry, acc = block(h, j, carry_ref[h], acc_ref[h], False)
            carry_ref[h] = carry
            acc_ref[h] = acc
        return j - 1, more_needed(jnp.maximum(j - 1, 0))

    lax.while_loop(cond, body, (i - 2, more_needed(jnp.maximum(i - 2, 0))))
    for h in range(HP):
        o_ref[:, hs(h)] = acc_ref[h].T.astype(o_ref.dtype)


def _attention(proj, vt):
    wide = HP * HEAD_DIM
    once = pl.Buffered(1)
    return pl.pallas_call(
        _attn_kernel,
        grid=(HEADS // HP, SEQ // TQ),
        in_specs=[
            pl.BlockSpec((TQ, wide), lambda g, i: (i, g)),
            pl.BlockSpec((SEQ, wide), lambda g, i: (0, HEADS // HP + g), pipeline_mode=once),
            pl.BlockSpec((HP, SEQ // TK, HEAD_DIM, TK), lambda g, i: (g, 0, 0, 0), pipeline_mode=once),
        ],
        out_specs=pl.BlockSpec((TQ, wide), lambda g, i: (i, g)),
        out_shape=jax.ShapeDtypeStruct((SEQ, SB_WIDTH), BF16),
        scratch_shapes=[
            pltpu.VMEM((SEQ // TK, HP, TQ), F32),
            pltpu.VMEM((HP, 1, TQ), F32),
            pltpu.VMEM((HP, HEAD_DIM, TQ), F32),
        ],
        compiler_params=_params("arbitrary", "arbitrary"),
        name="attention",
    )(proj, proj, vt)


def _ssm_weights(a_re, a_im, log_dt, b_re, b_im, c_re, c_im, d_skip):
    dt = jnp.exp(log_dt)[:, None]
    d = jnp.arange(CHUNK + 1, dtype=F32)[:, None, None]
    mag = jnp.exp(d * (dt * a_re))
    pw_re = mag * jnp.cos(d * (dt * a_im))
    pw_im = mag * jnp.sin(d * (dt * a_im))
    abar_re, abar_im = pw_re[1], pw_im[1]
    den = a_re * a_re + a_im * a_im
    num_re = abar_re - 1.0
    zoh_re = (num_re * a_re + abar_im * a_im) / den
    zoh_im = (abar_im * a_re - num_re * a_im) / den
    bb_re = zoh_re[..., None] * b_re - zoh_im[..., None] * b_im
    bb_im = zoh_re[..., None] * b_im + zoh_im[..., None] * b_re
    ca_re = c_re[None] * pw_re[:, :, None, :] - c_im[None] * pw_im[:, :, None, :]
    ca_im = c_re[None] * pw_im[:, :, None, :] + c_im[None] * pw_re[:, :, None, :]
    kd = jnp.einsum('dgon,gnp->dgop', ca_re, bb_re) - jnp.einsum('dgon,gnp->dgop', ca_im, bb_im)
    kd = kd.at[0].add(jnp.eye(SSM_GROUP, dtype=F32)[None] * d_skip[:, :, None])
    by_slab = lambda t: t.reshape((N_SLABS, SLAB_GROUPS) + t.shape[1:])
    k_rep = by_slab(kd[:CHUNK].transpose(1, 0, 3, 2)).transpose(0, 2, 3, 1, 4)
    k_rep = k_rep.reshape(N_SLABS, CHUNK, SSM_GROUP, SLAB)
    rev_re, rev_im = pw_re[CHUNK - 1::-1], pw_im[CHUNK - 1::-1]
    wz_re = rev_re[:, :, :, None] * bb_re[None] - rev_im[:, :, :, None] * bb_im[None]
    wz_im = rev_re[:, :, :, None] * bb_im[None] + rev_im[:, :, :, None] * bb_re[None]
    wz = jnp.stack([wz_re, wz_im], axis=0).transpose(2, 1, 4, 0, 3)
    wz_rep = by_slab(wz).transpose(0, 2, 3, 4, 1, 5).reshape(N_SLABS, CHUNK, SSM_GROUP, 2 * SLAB_STATE)
    v = jnp.stack([ca_re[1:], -ca_im[1:]], axis=0).transpose(2, 0, 4, 1, 3)
    v_rep = by_slab(v).transpose(0, 2, 3, 4, 1, 5).reshape(N_SLABS, 2, SSM_STATE, CHUNK * SLAB)
    lam = jnp.stack([pw_re[CHUNK], pw_im[CHUNK]], axis=1)
    lam = by_slab(lam).transpose(0, 2, 1, 3).reshape(N_SLABS, 2, SLAB_STATE)
    wz_mat, t_mat, v_mat = _ssm_expand(k_rep, wz_rep, v_rep)
    return wz_mat, t_mat, v_mat, lam


def _ssm_expand_kernel(k_ref, wz_ref, v_ref, wz_out, t_out, v_out):
    def own_group(rows, log2_per_row, cols, col_period, log2_per_col):
        a = lax.broadcasted_iota(jnp.int32, (rows, cols), 0) >> log2_per_row
        c = (lax.broadcasted_iota(jnp.int32, (rows, cols), 1) & (col_period - 1)) >> log2_per_col
        return a == c

    def expand(vals, mask):
        return jnp.where(mask, jnp.tile(vals, (SLAB_GROUPS, 1)), 0.0).astype(BF16)

    m_t = own_group(SLAB, 4, SLAB, SLAB, 4)
    lag_blocks = [expand(k_ref[d], m_t) for d in range(CHUNK)]
    for j in range(CHUNK):
        for b in range(CHUNK):
            blk = lag_blocks[b - j] if b >= j else jnp.zeros((SLAB, SLAB), BF16)
            t_out[j * SLAB:(j + 1) * SLAB, b * SLAB:(b + 1) * SLAB] = blk
    m_w = own_group(SLAB, 4, 2 * SLAB_STATE, SLAB_STATE, 6)
    for j in range(CHUNK):
        wz_out[j * SLAB:(j + 1) * SLAB, :] = expand(wz_ref[j], m_w)
    m_v = own_group(SLAB_STATE, 6, CHUNK * SLAB, SLAB, 4)
    for r in range(2):
        v_out[r * SLAB_STATE:(r + 1) * SLAB_STATE, :] = expand(v_ref[r], m_v)


def _ssm_expand(k_rep, wz_rep, v_rep):
    assert SSM_GROUP == 1 << 4 and SSM_STATE == 1 << 6
    mat = jax.ShapeDtypeStruct((N_SLABS, CHUNK * SLAB, CHUNK * SLAB), BF16)
    slab3 = lambda shape: pl.BlockSpec((None,) + shape, lambda k: (k, 0, 0))
    return pl.pallas_call(
        _ssm_expand_kernel,
        grid=(N_SLABS,),
        in_specs=[
            pl.BlockSpec((None, CHUNK, SSM_GROUP, SLAB), lambda k: (k, 0, 0, 0)),
            pl.BlockSpec((None, CHUNK, SSM_GROUP, 2 * SLAB_STATE), lambda k: (k, 0, 0, 0)),
            pl.BlockSpec((None, 2, SSM_STATE, CHUNK * SLAB), lambda k: (k, 0, 0, 0)),
        ],
        out_specs=[slab3((CHUNK * SLAB, 2 * SLAB_STATE)), slab3((CHUNK * SLAB, CHUNK * SLAB)),
                   slab3((2 * SLAB_STATE, CHUNK * SLAB))],
        out_shape=[mat, mat, mat],
        compiler_params=_params("parallel"),
        name="ssm_expand",
    )(k_rep, wz_rep, v_rep)


def _cmul(ar, ai, br, bi):
    return ar * br - ai * bi, ar * bi + ai * br


def _ssm_kernel(*refs):
    u_refs = refs[:CHUNK]
    wz_ref, t_ref, v_ref, lam_ref, o_ref, s_ref = refs[CHUNK:]
    ns = SLAB_STATE
    lam1 = (lam_ref[0:1, :], lam_ref[1:2, :])
    lam2 = _cmul(*lam1, *lam1)
    lam4 = _cmul(*lam2, *lam2)
    lam8 = _cmul(*lam4, *lam4)

    def lhs_rows(r0, n):
        return jnp.concatenate([u[pl.ds(r0, n), :] for u in u_refs], axis=1)

    def pass1(b, _):
        r0 = pl.multiple_of(b * SSM_ROWS, SSM_ROWS)
        z = jnp.dot(lhs_rows(r0, SSM_ROWS), wz_ref[...], preferred_element_type=F32)
        zr, zi = z[:, :ns], z[:, ns:]
        sub = lax.broadcasted_iota(jnp.int32, (SSM_ROWS, ns), 0) & (SUBLANES - 1)
        for shift, (cr, ci) in ((1, lam1), (2, lam2), (4, lam4)):
            keep = sub >= shift
            pr = jnp.where(keep, pltpu.roll(zr, shift, axis=0), 0.0)
            pi = jnp.where(keep, pltpu.roll(zi, shift, axis=0), 0.0)
            ar, ai = _cmul(cr, ci, pr, pi)
            zr, zi = zr + ar, zi + ai
        s_ref[pl.ds(r0, SSM_ROWS), :ns] = zr
        s_ref[pl.ds(r0, SSM_ROWS), ns:] = zi
        return 0

    lax.fori_loop(0, N_CHUNKS // SSM_ROWS, pass1, 0)

    pow_re, pow_im = [jnp.ones((1, ns), F32)], [jnp.zeros((1, ns), F32)]
    for _ in range(SUBLANES - 1):
        nr, ni = _cmul(pow_re[-1], pow_im[-1], *lam1)
        pow_re.append(nr)
        pow_im.append(ni)
    tab_re = jnp.concatenate(pow_re, axis=0)
    tab_im = jnp.concatenate(pow_im, axis=0)
    sub8 = lax.broadcasted_iota(jnp.int32, (SUBLANES, ns), 0)

    def pass2(t, c):
        cr, ci = c
        r0 = pl.multiple_of(t * SUBLANES, SUBLANES)
        lr = s_ref[pl.ds(r0, SUBLANES), :ns]
        li = s_ref[pl.ds(r0, SUBLANES), ns:]
        er = jnp.where(sub8 >= 1, pltpu.roll(lr, 1, axis=0), 0.0)
        ei = jnp.where(sub8 >= 1, pltpu.roll(li, 1, axis=0), 0.0)
        ar, ai = _cmul(tab_re, tab_im, cr, ci)
        s_ref[pl.ds(r0, SUBLANES), :ns] = er + ar
        s_ref[pl.ds(r0, SUBLANES), ns:] = ei + ai
        nr, ni = _cmul(*lam8, cr, ci)
        return lr[SUBLANES - 1:SUBLANES, :] + nr, li[SUBLANES - 1:SUBLANES, :] + ni

    zero = jnp.zeros((1, ns), F32)
    lax.fori_loop(0, N_CHUNKS // SUBLANES, pass2, (zero, zero))

    for b in range(N_CHUNKS // SSM_ROWS):
        r0 = b * SSM_ROWS
        y = jnp.dot(lhs_rows(r0, SSM_ROWS), t_ref[...], preferred_element_type=F32)
        y = y + jnp.dot(s_ref[pl.ds(r0, SSM_ROWS), :].astype(BF16), v_ref[...],
                        preferred_element_type=F32)
        for j in range(CHUNK):
            o_ref[pl.ds(r0 * CHUNK + j, SSM_ROWS, stride=CHUNK), :] = y[:, j * SLAB:(j + 1) * SLAB]


def _ssm(u2, wz_mat, t_mat, v_mat, lam):
    u_specs = [pl.BlockSpec((N_CHUNKS, SLAB), functools.partial(lambda j, k: (0, j * N_SLABS + k), j))
               for j in range(CHUNK)]
    return pl.pallas_call(
        _ssm_kernel,
        grid=(N_SLABS,),
        in_specs=u_specs + [
            pl.BlockSpec((None, CHUNK * SLAB, 2 * SLAB_STATE), lambda k: (k, 0, 0)),
            pl.BlockSpec((None, CHUNK * SLAB, CHUNK * SLAB), lambda k: (k, 0, 0)),
            pl.BlockSpec((None, 2 * SLAB_STATE, CHUNK * SLAB), lambda k: (k, 0, 0)),
            pl.BlockSpec((None, 2, SLAB_STATE), lambda k: (k, 0, 0)),
        ],
        out_specs=pl.BlockSpec((SEQ, SLAB), lambda k: (0, k), pipeline_mode=pl.Buffered(1)),
        out_shape=jax.ShapeDtypeStruct((SEQ, SSM_WIDTH), F32),
        scratch_shapes=[pltpu.VMEM((N_CHUNKS, 2 * SLAB_STATE), F32)],
        compiler_params=_params("arbitrary"),
        name="ssm",
    )(*([u2] * CHUNK), wz_mat, t_mat, v_mat, lam)


def _gelu_tanh(x):
    c = 0.7978845608028654
    return 0.5 * x * (1.0 + jnp.tanh(c * (x + 0.044715 * (x * x * x))))


def _post_kernel(x_ref, oa_ref, y_ref, ga_ref, gb_ref, wglu_ref, pa_ref, ps_ref, wo_ref, o_ref):
    y = _gelu_tanh(y_ref[...].astype(F32))
    yb = y.astype(BF16)
    glu = jnp.dot(yb, wglu_ref[...], preferred_element_type=F32)
    o_ssm = (y * _sigmoid(glu)).astype(BF16)
    a = jnp.dot(oa_ref[...], pa_ref[...], preferred_element_type=F32)
    b = jnp.dot(o_ssm, ps_ref[...], preferred_element_type=F32)
    merged = _sigmoid(ga_ref[...].astype(F32)) * a + _sigmoid(gb_ref[...].astype(F32)) * b
    o_ref[...] = x_ref[...] + jnp.dot(merged.astype(BF16), wo_ref[...], preferred_element_type=F32)


def _post(x, o_attn, y, proj, w_glu, p_attn, p_ssm, w_out):
    tm = 256
    ga_blk = (3 * SB_WIDTH + SSM_WIDTH) // D_MODEL
    const = lambda shape: pl.BlockSpec(shape, lambda i: (0, 0), pipeline_mode=pl.Buffered(1))
    return pl.pallas_call(
        _post_kernel,
        grid=(SEQ // tm,),
        in_specs=[
            pl.BlockSpec((tm, D_MODEL), lambda i: (i, 0)),
            pl.BlockSpec((tm, SB_WIDTH), lambda i: (i, 0)),
            pl.BlockSpec((tm, SSM_WIDTH), lambda i: (i, 0)),
            pl.BlockSpec((tm, D_MODEL), lambda i: (i, ga_blk)),
            pl.BlockSpec((tm, D_MODEL), lambda i: (i, ga_blk + 1)),
            const((SSM_WIDTH, SSM_WIDTH)),
            const((SB_WIDTH, D_MODEL)),
            const((SSM_WIDTH, D_MODEL)),
            const((D_MODEL, D_MODEL)),
        ],
        out_specs=pl.BlockSpec((tm, D_MODEL), lambda i: (i, 0)),
        out_shape=jax.ShapeDtypeStruct((SEQ, D_MODEL), F32),
        compiler_params=_params("parallel"),
        name="post",
    )(x, o_attn, y, proj, proj, w_glu, p_attn, p_ssm, w_out)


def _ffn_kernel(x_ref, g_ref, wg_ref, wu_ref, wd_ref, o_ref, h_ref):
    f = pl.program_id(1)

    @pl.when(f == 0)
    def _():
        x = x_ref[...]
        h_ref[...] = _rms_rows(x, g_ref[...]).astype(BF16)
        o_ref[...] = x

    h = h_ref[...]
    a = jnp.dot(h, wg_ref[...], preferred_element_type=F32)
    b = jnp.dot(h, wu_ref[...], preferred_element_type=F32)
    t = (a * _sigmoid(a) * b).astype(BF16)
    o_ref[...] += jnp.dot(t, wd_ref[...], preferred_element_type=F32)


def _ffn(x, g, w_gate, w_up, w_down):
    tm, tf = 512, 512
    return pl.pallas_call(
        _ffn_kernel,
        grid=(SEQ // tm, D_FF_DENSE // tf),
        in_specs=[
            pl.BlockSpec((tm, D_MODEL), lambda i, f: (i, 0)),
            pl.BlockSpec((1, D_MODEL), lambda i, f: (0, 0)),
            pl.BlockSpec((D_MODEL, tf), lambda i, f: (0, f)),
            pl.BlockSpec((D_MODEL, tf), lambda i, f: (0, f)),
            pl.BlockSpec((tf, D_MODEL), lambda i, f: (f, 0)),
        ],
        out_specs=pl.BlockSpec((tm, D_MODEL), lambda i, f: (i, 0)),
        out_shape=jax.ShapeDtypeStruct((SEQ, D_MODEL), F32),
        scratch_shapes=[pltpu.VMEM((tm, D_MODEL), BF16)],
        compiler_params=_params("parallel", "arbitrary"),
        name="ffn",
    )(x, g, w_gate, w_up, w_down)


def _moe_kernel(x_ref, g_ref, wr_ref, wg_ref, wu_ref, wd_ref, gf_ref, o_ref, h_ref, rw_ref, acc_ref):
    e = pl.program_id(1)
    f = pl.program_id(2)
    first = jnp.logical_and(e == 0, f == 0)
    last = jnp.logical_and(e == pl.num_programs(1) - 1, f == pl.num_programs(2) - 1)
    tm = x_ref.shape[0]
    lane = lax.broadcasted_iota(jnp.int32, (tm, LANES), 1)

    @pl.when(first)
    def _():
        x = x_ref[...]
        hf = _rms_rows(x, g_ref[...])
        h_ref[...] = hf.astype(BF16)
        acc_ref[...] = x
        logits = jnp.dot(hf, wr_ref[...], preferred_element_type=F32, precision=lax.Precision.HIGHEST)
        lg = jnp.where(lane < N_EXPERTS, logits, -jnp.inf)
        m1 = jnp.max(lg, axis=-1, keepdims=True)
        i1 = jnp.min(jnp.where(lg == m1, lane, LANES), axis=-1, keepdims=True)
        lg2 = jnp.where(lane == i1, -jnp.inf, lg)
        m2 = jnp.max(lg2, axis=-1, keepdims=True)
        i2 = jnp.min(jnp.where(lg2 == m2, lane, LANES), axis=-1, keepdims=True)
        e2 = jnp.exp(m2 - m1)
        g1 = 1.0 / (1.0 + e2)
        g2 = e2 / (1.0 + e2)
        rw_ref[...] = jnp.where(lane == i1, g1, 0.0) + jnp.where(lane == i2, g2, 0.0)

    h = h_ref[...]
    we = jnp.sum(jnp.where(lane == e, rw_ref[...], 0.0), axis=-1, keepdims=True)
    a = jnp.dot(h, wg_ref[...], preferred_element_type=F32)
    b = jnp.dot(h, wu_ref[...], preferred_element_type=F32)
    t = (a * _sigmoid(a) * b * we).astype(BF16)
    acc_ref[...] += jnp.dot(t, wd_ref[...], preferred_element_type=F32)

    @pl.when(last)
    def _():
        o_ref[...] = _rms_rows(acc_ref[...], gf_ref[...])


def _moe(x, g, w_router, w_gate, w_up, w_down, g_final):
    tm, tf = 512, 512
    return pl.pallas_call(
        _moe_kernel,
        grid=(SEQ // tm, N_EXPERTS, D_FF_EXPERT // tf),
        in_specs=[
            pl.BlockSpec((tm, D_MODEL), lambda i, e, f: (i, 0)),
            pl.BlockSpec((1, D_MODEL), lambda i, e, f: (0, 0)),
            pl.BlockSpec((D_MODEL, LANES), lambda i, e, f: (0, 0)),
            pl.BlockSpec((None, D_MODEL, tf), lambda i, e, f: (e, 0, f)),
            pl.BlockSpec((None, D_MODEL, tf), lambda i, e, f: (e, 0, f)),
            pl.BlockSpec((None, tf, D_MODEL), lambda i, e, f: (e, f, 0)),
            pl.BlockSpec((1, D_MODEL), lambda i, e, f: (0, 0)),
        ],
        out_specs=pl.BlockSpec((tm, D_MODEL), lambda i, e, f: (i, 0)),
        out_shape=jax.ShapeDtypeStruct((SEQ, D_MODEL), F32),
        scratch_shapes=[
            pltpu.VMEM((tm, D_MODEL), BF16),
            pltpu.VMEM((tm, LANES), F32),
            pltpu.VMEM((tm, D_MODEL), F32),
        ],
        compiler_params=_params("parallel", "arbitrary", "arbitrary"),
        name="moe",
    )(x, g, w_router, w_gate, w_up, w_down, g_final)


def kernel(x, mix_norm, ffn_norm, final_norm, w_in, ssm_a_re, ssm_a_im, ssm_log_dt, ssm_b_re, ssm_b_im,
           ssm_c_re, ssm_c_im, ssm_d, w_glu, p_attn, p_ssm, w_out, ffn_w_gate, ffn_w_up, ffn_w_down,
           w_router, moe_w_gate, moe_w_up, moe_w_down):
    assert x.shape == (1, SEQ, D_MODEL) and DEPTH == 2
    xs = x.reshape(SEQ, D_MODEL)
    col_scale = jnp.concatenate([jnp.full((SB_WIDTH,), HEAD_DIM ** -0.5 * LOG2E, F32),
                                 jnp.ones((IN_COLS - SB_WIDTH,), F32)])
    for layer in range(DEPTH):
        w_in_b = (w_in[layer] * col_scale).astype(BF16)
        proj, u, vt = _inproj(xs, mix_norm[layer][None], w_in_b)
        o_attn = _attention(proj, vt)
        ssm_w = _ssm_weights(ssm_a_re[layer], ssm_a_im[layer], ssm_log_dt[layer], ssm_b_re[layer],
                             ssm_b_im[layer], ssm_c_re[layer], ssm_c_im[layer], ssm_d[layer])
        y = _ssm(u, *ssm_w)
        xs = _post(xs, o_attn, y, proj, w_glu[layer].astype(BF16), p_attn[layer].astype(BF16),
                   p_ssm[layer].astype(BF16), w_out[layer].astype(BF16))
        i = layer // 2
        if layer % 2 == 0:
            xs = _ffn(xs, ffn_norm[layer][None], ffn_w_gate[i].astype(BF16), ffn_w_up[i].astype(BF16),
                      ffn_w_down[i].astype(BF16))
        else:
            wr = jnp.pad(w_router[i], ((0, 0), (0, LANES - N_EXPERTS)))
            xs = _moe(xs, ffn_norm[layer][None], wr, moe_w_gate[i].astype(BF16), moe_w_up[i].astype(BF16),
                      moe_w_down[i].astype(BF16), final_norm[None])
    return xs.reshape(1, SEQ, D_MODEL)
```

```python
import functools

import jax
import jax.numpy as jnp
from jax import lax
from jax.experimental import pallas as pl
from jax.experimental.pallas import tpu as pltpu

F32 = jnp.float32
BF16 = jnp.bfloat16

D_MODEL = 2048
SEQ = 16384
DEPTH = 2
HEAD_DIM = 128
HEADS = 8
SB_WIDTH = HEADS * HEAD_DIM
SSM_WIDTH = 1024
SSM_GROUP = 16
SSM_GROUPS = 64
SSM_STATE = 64
IN_COLS = 3 * SB_WIDTH + SSM_WIDTH + 2 * D_MODEL
D_FF_DENSE = 2 * D_MODEL
N_EXPERTS = 8
D_FF_EXPERT = D_MODEL // 2
RMS_EPS = 1e-6

LANES = 128
SUBLANES = 8
VMEM_LIMIT = 56 * 1024 * 1024

TQ = 256
TK = 256
HP = 4
LOG2E = 1.4426950408889634
UNDERFLOW_LOG2 = 152.0
NORM_MARGIN = 1.01
CHUNK = 8
N_CHUNKS = SEQ // CHUNK
SLAB = LANES
N_SLABS = SSM_WIDTH // SLAB
SLAB_GROUPS = SLAB // SSM_GROUP
SLAB_STATE = SLAB_GROUPS * SSM_STATE
SSM_ROWS = 256


def _params(*sem):
    return pltpu.CompilerParams(dimension_semantics=sem, vmem_limit_bytes=VMEM_LIMIT)


def _rms_rows(x, g):
    ms = jnp.mean(x * x, axis=-1, keepdims=True)
    return x * lax.rsqrt(ms + RMS_EPS) * g


def _sigmoid(x):
    return 1.0 / (1.0 + jnp.exp(-x))


def _inproj_kernel(x_ref, g_ref, w_ref, p_ref, u_ref, vt_ref, h_ref, s_ref):
    j = pl.program_id(1)

    @pl.when(j == 0)
    def _():
        h_ref[...] = _rms_rows(x_ref[...], g_ref[...]).astype(BF16)

    r = jnp.dot(h_ref[...], w_ref[...], preferred_element_type=F32)
    p_ref[...] = r.astype(BF16)

    @pl.when(j == 2)
    def _():
        rt = r.T
        for h in range(HEADS):
            for b in range(vt_ref.shape[1]):
                vt_ref[h, b] = rt[h * HEAD_DIM:(h + 1) * HEAD_DIM, b * TK:(b + 1) * TK].astype(BF16)

    @pl.when(j == 3)
    def _():
        slabs = s_ref.shape[0]
        rows = s_ref.shape[1] // CHUNK
        for l in range(slabs):
            s_ref[l] = r[:, l * LANES:(l + 1) * LANES]
        u_ref[...] = jnp.concatenate(
            [s_ref[l, pl.ds(c, rows, stride=CHUNK), :] for c in range(CHUNK) for l in range(slabs)],
            axis=1).astype(BF16)


def _inproj(x, g, w):
    tm, tn = 512, 1024
    assert tn == SB_WIDTH == SSM_WIDTH and tm % TK == 0 and tm % CHUNK == 0
    return pl.pallas_call(
        _inproj_kernel,
        grid=(SEQ // tm, IN_COLS // tn),
        in_specs=[
            pl.BlockSpec((tm, D_MODEL), lambda i, j: (i, 0)),
            pl.BlockSpec((1, D_MODEL), lambda i, j: (0, 0)),
            pl.BlockSpec((D_MODEL, tn), lambda i, j: (0, j)),
        ],
        out_specs=[
            pl.BlockSpec((tm, tn), lambda i, j: (i, j)),
            pl.BlockSpec((tm // CHUNK, CHUNK * SSM_WIDTH), lambda i, j: (i, 0)),
            pl.BlockSpec((HEADS, tm // TK, HEAD_DIM, TK), lambda i, j: (0, i, 0, 0)),
        ],
        out_shape=[
            jax.ShapeDtypeStruct((SEQ, IN_COLS), BF16),
            jax.ShapeDtypeStruct((N_CHUNKS, CHUNK * SSM_WIDTH), BF16),
            jax.ShapeDtypeStruct((HEADS, SEQ // TK, HEAD_DIM, TK), BF16),
        ],
        scratch_shapes=[pltpu.VMEM((tm, D_MODEL), BF16), pltpu.VMEM((SSM_WIDTH // LANES, tm, LANES), F32)],
        compiler_params=_params("parallel", "arbitrary"),
        name="inproj",
    )(x, g, w)


def _attn_kernel(q_ref, k_ref, vt_ref, o_ref, kmax_ref, carry_ref, acc_ref):
    i = pl.program_id(1)
    hs = lambda h: slice(h * HEAD_DIM, (h + 1) * HEAD_DIM)

    @pl.when(i == 0)
    def _():
        def body(j, run):
            ks = pl.multiple_of(j * TK, TK)
            new, rows = [], []
            for h in range(HP):
                kb = k_ref[pl.ds(ks, TK), hs(h)].astype(F32)
                n2 = jnp.max(jnp.sum(kb * kb, axis=1, keepdims=True), axis=0, keepdims=True)
                new.append(jnp.maximum(run[h], n2))
                rows.append(jnp.broadcast_to(jnp.sqrt(new[h]), (1, TQ)))
            kmax_ref[j] = jnp.concatenate(rows, axis=0)
            return tuple(new)

        lax.fori_loop(0, SEQ // TK, body, tuple(jnp.zeros((1, 1), F32) for _ in range(HP)))

    row = lax.broadcasted_iota(jnp.int32, (TK, TQ), 0)
    col = lax.broadcasted_iota(jnp.int32, (TK, TQ), 1)
    upper = jnp.where(col >= row, 1.0, 0.0).astype(BF16)
    causal = row < col

    def block(h, j, carry, acc, masked):
        ks = pl.multiple_of(j * TK, TK)
        kb = k_ref[pl.ds(ks, TK), hs(h)]
        zt = lax.dot_general(kb, q_ref[:, hs(h)], (((1,), (1,)), ((), ())),
                             preferred_element_type=F32)
        sp = jnp.maximum(zt, 0.0) + jnp.log(1.0 + jnp.exp2(-jnp.abs(zt))) * LOG2E
        if masked:
            sp = jnp.where(causal, sp, 0.0)
        rt = jnp.dot(upper, sp.astype(BF16), preferred_element_type=F32)
        w = jnp.exp2(zt - rt - carry)
        if masked:
            w = jnp.where(causal, w, 0.0)
        acc = acc + jnp.dot(vt_ref[h, j], w.astype(BF16), preferred_element_type=F32)
        return carry + rt[0:1, :], acc

    prev = jnp.maximum(i - 1, 0)
    no_prev = jnp.where(i == 0, 1e30, 0.0).astype(F32)
    qmax = []
    for h in range(HP):
        carry, acc = block(h, i, jnp.zeros((1, TQ), F32), jnp.zeros((HEAD_DIM, TQ), F32), True)
        carry, acc = block(h, prev, carry + no_prev, acc, False)
        carry_ref[h] = carry
        acc_ref[h] = acc
        qf = q_ref[:, hs(h)].astype(F32)
        n2 = jnp.max(jnp.sum(qf * qf, axis=1, keepdims=True), axis=0, keepdims=True)
        qmax.append(jnp.sqrt(n2) * NORM_MARGIN)

    def more_needed(j):
        km = kmax_ref[j]
        worst = None
        for h in range(HP):
            slack = carry_ref[h] - (UNDERFLOW_LOG2 + qmax[h] * km[h:h + 1, :])
            worst = slack if worst is None else jnp.minimum(worst, slack)
        return (jnp.min(worst) < 0.0).astype(jnp.int32)

    def cond(c):
        return jnp.logical_and(c[0] >= 0, c[1] > 0)

    def body(c):
        j = c[0]
        for h in range(HP):
            carry, acc = block(h, j, carry_ref[h], acc_ref[h], False)
            carry_ref[h] = carry
            acc_ref[h] = acc
        return j - 1, more_needed(jnp.maximum(j - 1, 0))

    lax.while_loop(cond, body, (i - 2, more_needed(jnp.maximum(i - 2, 0))))
    for h in range(HP):
        o_ref[:, hs(h)] = acc_ref[h].T.astype(o_ref.dtype)


def _attention(proj, vt):
    wide = HP * HEAD_DIM
    once = pl.Buffered(1)
    return pl.pallas_call(
        _attn_kernel,
        grid=(HEADS // HP, SEQ // TQ),
        in_specs=[
            pl.BlockSpec((TQ, wide), lambda g, i: (i, g)),
            pl.BlockSpec((SEQ, wide), lambda g, i: (0, HEADS // HP + g), pipeline_mode=once),
            pl.BlockSpec((HP, SEQ // TK, HEAD_DIM, TK), lambda g, i: (g, 0, 0, 0), pipeline_mode=once),
        ],
        out_specs=pl.BlockSpec((TQ, wide), lambda g, i: (i, g)),
        out_shape=jax.ShapeDtypeStruct((SEQ, SB_WIDTH), BF16),
        scratch_shapes=[
            pltpu.VMEM((SEQ // TK, HP, TQ), F32),
            pltpu.VMEM((HP, 1, TQ), F32),
            pltpu.VMEM((HP, HEAD_DIM, TQ), F32),
        ],
        compiler_params=_params("arbitrary", "arbitrary"),
        name="attention",
    )(proj, proj, vt)


def _ssm_weights(a_re, a_im, log_dt, b_re, b_im, c_re, c_im, d_skip):
    dt = jnp.exp(log_dt)[:, None]
    d = jnp.arange(CHUNK + 1, dtype=F32)[:, None, None]
    mag = jnp.exp(d * (dt * a_re))
    pw_re = mag * jnp.cos(d * (dt * a_im))
    pw_im = mag * jnp.sin(d * (dt * a_im))
    abar_re, abar_im = pw_re[1], pw_im[1]
    den = a_re * a_re + a_im * a_im
    num_re = abar_re - 1.0
    zoh_re = (num_re * a_re + abar_im * a_im) / den
    zoh_im = (abar_im * a_re - num_re * a_im) / den
    bb_re = zoh_re[..., None] * b_re - zoh_im[..., None] * b_im
    bb_im = zoh_re[..., None] * b_im + zoh_im[..., None] * b_re
    ca_re = c_re[None] * pw_re[:, :, None, :] - c_im[None] * pw_im[:, :, None, :]
    ca_im = c_re[None] * pw_im[:, :, None, :] + c_im[None] * pw_re[:, :, None, :]
    kd = jnp.einsum('dgon,gnp->dgop', ca_re, bb_re) - jnp.einsum('dgon,gnp->dgop', ca_im, bb_im)
    kd = kd.at[0].add(jnp.eye(SSM_GROUP, dtype=F32)[None] * d_skip[:, :, None])
    by_slab = lambda t: t.reshape((N_SLABS, SLAB_GROUPS) + t.shape[1:])
    k_rep = by_slab(kd[:CHUNK].transpose(1, 0, 3, 2)).transpose(0, 2, 3, 1, 4)
    k_rep = k_rep.reshape(N_SLABS, CHUNK, SSM_GROUP, SLAB)
    rev_re, rev_im = pw_re[CHUNK - 1::-1], pw_im[CHUNK - 1::-1]
    wz_re = rev_re[:, :, :, None] * bb_re[None] - rev_im[:, :, :, None] * bb_im[None]
    wz_im = rev_re[:, :, :, None] * bb_im[None] + rev_im[:, :, :, None] * bb_re[None]
    wz = jnp.stack([wz_re, wz_im], axis=0).transpose(2, 1, 4, 0, 3)
    wz_rep = by_slab(wz).transpose(0, 2, 3, 4, 1, 5).reshape(N_SLABS, CHUNK, SSM_GROUP, 2 * SLAB_STATE)
    v = jnp.stack([ca_re[1:], -ca_im[1:]], axis=0).transpose(2, 0, 4, 1, 3)
    v_rep = by_slab(v).transpose(0, 2, 3, 4, 1, 5).reshape(N_SLABS, 2, SSM_STATE, CHUNK * SLAB)
    lam = jnp.stack([pw_re[CHUNK], pw_im[CHUNK]], axis=1)
    lam = by_slab(lam).transpose(0, 2, 1, 3).reshape(N_SLABS, 2, SLAB_STATE)
    wz_mat, t_mat, v_mat = _ssm_expand(k_rep, wz_rep, v_rep)
    return wz_mat, t_mat, v_mat, lam


def _ssm_expand_kernel(k_ref, wz_ref, v_ref, wz_out, t_out, v_out):
    def own_group(rows, log2_per_row, cols, col_period, log2_per_col):
        a = lax.broadcasted_iota(jnp.int32, (rows, cols), 0) >> log2_per_row
        c = (lax.broadcasted_iota(jnp.int32, (rows, cols), 1) & (col_period - 1)) >> log2_per_col
        return a == c

    def expand(vals, mask):
        return jnp.where(mask, jnp.tile(vals, (SLAB_GROUPS, 1)), 0.0).astype(BF16)

    m_t = own_group(SLAB, 4, SLAB, SLAB, 4)
    lag_blocks = [expand(k_ref[d], m_t) for d in range(CHUNK)]
    for j in range(CHUNK):
        for b in range(CHUNK):
            blk = lag_blocks[b - j] if b >= j else jnp.zeros((SLAB, SLAB), BF16)
            t_out[j * SLAB:(j + 1) * SLAB, b * SLAB:(b + 1) * SLAB] = blk
    m_w = own_group(SLAB, 4, 2 * SLAB_STATE, SLAB_STATE, 6)
    for j in range(CHUNK):
        wz_out[j * SLAB:(j + 1) * SLAB, :] = expand(wz_ref[j], m_w)
    m_v = own_group(SLAB_STATE, 6, CHUNK * SLAB, SLAB, 4)
    for r in range(2):
        v_out[r * SLAB_STATE:(r + 1) * SLAB_STATE, :] = expand(v_ref[r], m_v)


def _ssm_expand(k_rep, wz_rep, v_rep):
    assert SSM_GROUP == 1 << 4 and SSM_STATE == 1 << 6
    mat = jax.ShapeDtypeStruct((N_SLABS, CHUNK * SLAB, CHUNK * SLAB), BF16)
    slab3 = lambda shape: pl.BlockSpec((None,) + shape, lambda k: (k, 0, 0))
    return pl.pallas_call(
        _ssm_expand_kernel,
        grid=(N_SLABS,),
        in_specs=[
            pl.BlockSpec((None, CHUNK, SSM_GROUP, SLAB), lambda k: (k, 0, 0, 0)),
            pl.BlockSpec((None, CHUNK, SSM_GROUP, 2 * SLAB_STATE), lambda k: (k, 0, 0, 0)),
            pl.BlockSpec((None, 2, SSM_STATE, CHUNK * SLAB), lambda k: (k, 0, 0, 0)),
        ],
        out_specs=[slab3((CHUNK * SLAB, 2 * SLAB_STATE)), slab3((CHUNK * SLAB, CHUNK * SLAB)),
                   slab3((2 * SLAB_STATE, CHUNK * SLAB))],
        out_shape=[mat, mat, mat],
        compiler_params=_params("parallel"),
        name="ssm_expand",
    )(k_rep, wz_rep, v_rep)


def _cmul(ar, ai, br, bi):
    return ar * br - ai * bi, ar * bi + ai * br


def _ssm_kernel(*refs):
    u_refs = refs[:CHUNK]
    wz_ref, t_ref, v_ref, lam_ref, o_ref, s_ref = refs[CHUNK:]
    ns = SLAB_STATE
    lam1 = (lam_ref[0:1, :], lam_ref[1:2, :])
    lam2 = _cmul(*lam1, *lam1)
    lam4 = _cmul(*lam2, *lam2)
    lam8 = _cmul(*lam4, *lam4)

    def lhs_rows(r0, n):
        return jnp.concatenate([u[pl.ds(r0, n), :] for u in u_refs], axis=1)

    def pass1(b, _):
        r0 = pl.multiple_of(b * SSM_ROWS, SSM_ROWS)
        z = jnp.dot(lhs_rows(r0, SSM_ROWS), wz_ref[...], preferred_element_type=F32)
        zr, zi = z[:, :ns], z[:, ns:]
        sub = lax.broadcasted_iota(jnp.int32, (SSM_ROWS, ns), 0) & (SUBLANES - 1)
        for shift, (cr, ci) in ((1, lam1), (2, lam2), (4, lam4)):
            keep = sub >= shift
            pr = jnp.where(keep, pltpu.roll(zr, shift, axis=0), 0.0)
            pi = jnp.where(keep, pltpu.roll(zi, shift, axis=0), 0.0)
            ar, ai = _cmul(cr, ci, pr, pi)
            zr, zi = zr + ar, zi + ai
        s_ref[pl.ds(r0, SSM_ROWS), :ns] = zr
        s_ref[pl.ds(r0, SSM_ROWS), ns:] = zi
        return 0

    lax.fori_loop(0, N_CHUNKS // SSM_ROWS, pass1, 0)

    pow_re, pow_im = [jnp.ones((1, ns), F32)], [jnp.zeros((1, ns), F32)]
    for _ in range(SUBLANES - 1):
        nr, ni = _cmul(pow_re[-1], pow_im[-1], *lam1)
        pow_re.append(nr)
        pow_im.append(ni)
    tab_re = jnp.concatenate(pow_re, axis=0)
    tab_im = jnp.concatenate(pow_im, axis=0)
    sub8 = lax.broadcasted_iota(jnp.int32, (SUBLANES, ns), 0)

    def pass2(t, c):
        cr, ci = c
        r0 = pl.multiple_of(t * SUBLANES, SUBLANES)
        lr = s_ref[pl.ds(r0, SUBLANES), :ns]
        li = s_ref[pl.ds(r0, SUBLANES), ns:]
        er = jnp.where(sub8 >= 1, pltpu.roll(lr, 1, axis=0), 0.0)
        ei = jnp.where(sub8 >= 1, pltpu.roll(li, 1, axis=0), 0.0)
        ar, ai = _cmul(tab_re, tab_im, cr, ci)
        s_ref[pl.ds(r0, SUBLANES), :ns] = er + ar
        s_ref[pl.ds(r0, SUBLANES), ns:] = ei + ai
        nr, ni = _cmul(*lam8, cr, ci)
        return lr[SUBLANES - 1:SUBLANES, :] + nr, li[SUBLANES - 1:SUBLANES, :] + ni

    zero = jnp.zeros((1, ns), F32)
    lax.fori_loop(0, N_CHUNKS // SUBLANES, pass2, (zero, zero))

    for b in range(N_CHUNKS // SSM_ROWS):
        r0 = b * SSM_ROWS
        y = jnp.dot(lhs_rows(r0, SSM_ROWS), t_ref[...], preferred_element_type=F32)
        y = y + jnp.dot(s_ref[pl.ds(r0, SSM_ROWS), :].astype(BF16), v_ref[...],
                        preferred_element_type=F32)
        for j in range(CHUNK):
            o_ref[pl.ds(r0 * CHUNK + j, SSM_ROWS, stride=CHUNK), :] = y[:, j * SLAB:(j + 1) * SLAB]


def _ssm(u2, wz_mat, t_mat, v_mat, lam):
    u_specs = [pl.BlockSpec((N_CHUNKS, SLAB), functools.partial(lambda j, k: (0, j * N_SLABS + k), j))
               for j in range(CHUNK)]
    return pl.pallas_call(
        _ssm_kernel,
        grid=(N_SLABS,),
        in_specs=u_specs + [
            pl.BlockSpec((None, CHUNK * SLAB, 2 * SLAB_STATE), lambda k: (k, 0, 0)),
            pl.BlockSpec((None, CHUNK * SLAB, CHUNK * SLAB), lambda k: (k, 0, 0)),
            pl.BlockSpec((None, 2 * SLAB_STATE, CHUNK * SLAB), lambda k: (k, 0, 0)),
            pl.BlockSpec((None, 2, SLAB_STATE), lambda k: (k, 0, 0)),
        ],
        out_specs=pl.BlockSpec((SEQ, SLAB), lambda k: (0, k), pipeline_mode=pl.Buffered(1)),
        out_shape=jax.ShapeDtypeStruct((SEQ, SSM_WIDTH), F32),
        scratch_shapes=[pltpu.VMEM((N_CHUNKS, 2 * SLAB_STATE), F32)],
        compiler_params=_params("arbitrary"),
        name="ssm",
    )(*([u2] * CHUNK), wz_mat, t_mat, v_mat, lam)


def _gelu_tanh(x):
    c = 0.7978845608028654
    return 0.5 * x * (1.0 + jnp.tanh(c * (x + 0.044715 * (x * x * x))))


def _post_kernel(x_ref, oa_ref, y_ref, ga_ref, gb_ref, wglu_ref, pa_ref, ps_ref, wo_ref, o_ref):
    y = _gelu_tanh(y_ref[...].astype(F32))
    yb = y.astype(BF16)
    glu = jnp.dot(yb, wglu_ref[...], preferred_element_type=F32)
    o_ssm = (y * _sigmoid(glu)).astype(BF16)
    a = jnp.dot(oa_ref[...], pa_ref[...], preferred_element_type=F32)
    b = jnp.dot(o_ssm, ps_ref[...], preferred_element_type=F32)
    merged = _sigmoid(ga_ref[...].astype(F32)) * a + _sigmoid(gb_ref[...].astype(F32)) * b
    o_ref[...] = x_ref[...] + jnp.dot(merged.astype(BF16), wo_ref[...], preferred_element_type=F32)


def _post(x, o_attn, y, proj, w_glu, p_attn, p_ssm, w_out):
    tm = 256
    ga_blk = (3 * SB_WIDTH + SSM_WIDTH) // D_MODEL
    const = lambda shape: pl.BlockSpec(shape, lambda i: (0, 0), pipeline_mode=pl.Buffered(1))
    return pl.pallas_call(
        _post_kernel,
        grid=(SEQ // tm,),
        in_specs=[
            pl.BlockSpec((tm, D_MODEL), lambda i: (i, 0)),
            pl.BlockSpec((tm, SB_WIDTH), lambda i: (i, 0)),
            pl.BlockSpec((tm, SSM_WIDTH), lambda i: (i, 0)),
            pl.BlockSpec((tm, D_MODEL), lambda i: (i, ga_blk)),
            pl.BlockSpec((tm, D_MODEL), lambda i: (i, ga_blk + 1)),
            const((SSM_WIDTH, SSM_WIDTH)),
            const((SB_WIDTH, D_MODEL)),
            const((SSM_WIDTH, D_MODEL)),
            const((D_MODEL, D_MODEL)),
        ],
        out_specs=pl.BlockSpec((tm, D_MODEL), lambda i: (i, 0)),
        out_shape=jax.ShapeDtypeStruct((SEQ, D_MODEL), F32),
        compiler_params=_params("parallel"),
        name="post",
    )(x, o_attn, y, proj, proj, w_glu, p_attn, p_ssm, w_out)


def _ffn_kernel(x_ref, g_ref, wg_ref, wu_ref, wd_ref, o_ref, h_ref):
    f = pl.program_id(1)

    @pl.when(f == 0)
    def _():
        x = x_ref[...]
        h_ref[...] = _rms_rows(x, g_ref[...]).astype(BF16)
        o_ref[...] = x

    h = h_ref[...]
    a = jnp.dot(h, wg_ref[...], preferred_element_type=F32)
    b = jnp.dot(h, wu_ref[...], preferred_element_type=F32)
    t = (a * _sigmoid(a) * b).astype(BF16)
    o_ref[...] += jnp.dot(t, wd_ref[...], preferred_element_type=F32)


def _ffn(x, g, w_gate, w_up, w_down):
    tm, tf = 512, 512
    return pl.pallas_call(
        _ffn_kernel,
        grid=(SEQ // tm, D_FF_DENSE // tf),
        in_specs=[
            pl.BlockSpec((tm, D_MODEL), lambda i, f: (i, 0)),
            pl.BlockSpec((1, D_MODEL), lambda i, f: (0, 0)),
            pl.BlockSpec((D_MODEL, tf), lambda i, f: (0, f)),
            pl.BlockSpec((D_MODEL, tf), lambda i, f: (0, f)),
            pl.BlockSpec((tf, D_MODEL), lambda i, f: (f, 0)),
        ],
        out_specs=pl.BlockSpec((tm, D_MODEL), lambda i, f: (i, 0)),
        out_shape=jax.ShapeDtypeStruct((SEQ, D_MODEL), F32),
        scratch_shapes=[pltpu.VMEM((tm, D_MODEL), BF16)],
        compiler_params=_params("parallel", "arbitrary"),
        name="ffn",
    )(x, g, w_gate, w_up, w_down)


MOE_TM = 512
MOE_ROWS = 160
MOE_PAD = 256


def _moe_kernel(x_ref, g_ref, wr_ref, wg_ref, wu_ref, wd_ref, gf_ref, o_ref,
                h_ref, rw_ref, rk_ref, rkt_ref, cnt_ref, y_ref, acc_ref):
    e = pl.program_id(1)
    tm = x_ref.shape[0]
    lane = lax.broadcasted_iota(jnp.int32, (tm, LANES), 1)

    @pl.when(e == 0)
    def _():
        x = x_ref[...]
        hf = _rms_rows(x, g_ref[...])
        h_ref[...] = hf.astype(BF16)
        acc_ref[...] = x
        logits = jnp.dot(hf, wr_ref[...], preferred_element_type=F32, precision=lax.Precision.HIGHEST)
        lg = jnp.where(lane < N_EXPERTS, logits, -jnp.inf)
        m1 = jnp.max(lg, axis=-1, keepdims=True)
        i1 = jnp.min(jnp.where(lg == m1, lane, LANES), axis=-1, keepdims=True)
        lg2 = jnp.where(lane == i1, -jnp.inf, lg)
        m2 = jnp.max(lg2, axis=-1, keepdims=True)
        i2 = jnp.min(jnp.where(lg2 == m2, lane, LANES), axis=-1, keepdims=True)
        e2 = jnp.exp(m2 - m1)
        g1 = 1.0 / (1.0 + e2)
        g2 = e2 / (1.0 + e2)
        rw_ref[...] = jnp.where(lane == i1, g1, 0.0) + jnp.where(lane == i2, g2, 0.0)
        routed = jnp.logical_or(lane == i1, lane == i2)
        onehot = jnp.where(routed, 1.0, 0.0)
        before = (lax.broadcasted_iota(jnp.int32, (tm, tm), 1)
                  < lax.broadcasted_iota(jnp.int32, (tm, tm), 0))
        ahead = jnp.dot(jnp.where(before, 1.0, 0.0).astype(BF16), onehot.astype(BF16),
                        preferred_element_type=F32)
        rk = jnp.where(routed, ahead, -1.0)
        rk_ref[...] = rk
        rkt_ref[...] = rk.T
        cnt_ref[...] = jnp.sum(onehot, axis=0, keepdims=True)
        y_ref[...] = jnp.zeros_like(y_ref)

    pick = lambda ref: jnp.sum(jnp.where(lane == e, ref[...], 0.0), axis=-1, keepdims=True)
    rk_col, gate_col = pick(rk_ref), pick(rw_ref)
    rk_row = rkt_ref[pl.ds(e, 1), :]
    lane1 = lax.broadcasted_iota(jnp.int32, (1, LANES), 1)
    count = jnp.sum(jnp.where(lane1 == e, cnt_ref[...], 0.0)).astype(jnp.int32)
    row_id = lax.broadcasted_iota(jnp.int32, (MOE_ROWS, tm), 0).astype(F32)
    col_id = lax.broadcasted_iota(jnp.int32, (tm, MOE_PAD), 1).astype(F32)

    def one_pass(s, _):
        base = (s * MOE_ROWS).astype(F32)
        sel = jnp.where(rk_row - base == row_id, 1.0, 0.0).astype(BF16)
        xc = jnp.dot(sel, h_ref[...], preferred_element_type=F32).astype(BF16)
        a = jnp.dot(xc, wg_ref[...], preferred_element_type=F32)
        b = jnp.dot(xc, wu_ref[...], preferred_element_type=F32)
        t = (a * _sigmoid(a) * b).astype(BF16)
        y_ref[:MOE_ROWS, :] = jnp.dot(t, wd_ref[...], preferred_element_type=F32).astype(BF16)
        back = jnp.where(rk_col - base == col_id, 1.0, 0.0).astype(BF16)
        acc_ref[...] += gate_col * jnp.dot(back, y_ref[...], preferred_element_type=F32)
        return 0

    lax.fori_loop(0, (count + MOE_ROWS - 1) // MOE_ROWS, one_pass, 0)

    @pl.when(e == pl.num_programs(1) - 1)
    def _():
        o_ref[...] = _rms_rows(acc_ref[...], gf_ref[...])


def _moe(x, g, w_router, w_gate, w_up, w_down, g_final):
    tm = MOE_TM
    return pl.pallas_call(
        _moe_kernel,
        grid=(SEQ // tm, N_EXPERTS),
        in_specs=[
            pl.BlockSpec((tm, D_MODEL), lambda i, e: (i, 0)),
            pl.BlockSpec((1, D_MODEL), lambda i, e: (0, 0)),
            pl.BlockSpec((D_MODEL, LANES), lambda i, e: (0, 0)),
            pl.BlockSpec((None, D_MODEL, D_FF_EXPERT), lambda i, e: (e, 0, 0)),
            pl.BlockSpec((None, D_MODEL, D_FF_EXPERT), lambda i, e: (e, 0, 0)),
            pl.BlockSpec((None, D_FF_EXPERT, D_MODEL), lambda i, e: (e, 0, 0)),
            pl.BlockSpec((1, D_MODEL), lambda i, e: (0, 0)),
        ],
        out_specs=pl.BlockSpec((tm, D_MODEL), lambda i, e: (i, 0)),
        out_shape=jax.ShapeDtypeStruct((SEQ, D_MODEL), F32),
        scratch_shapes=[
            pltpu.VMEM((tm, D_MODEL), BF16),
            pltpu.VMEM((tm, LANES), F32),
            pltpu.VMEM((tm, LANES), F32),
            pltpu.VMEM((LANES, tm), F32),
            pltpu.VMEM((1, LANES), F32),
            pltpu.VMEM((MOE_PAD, D_MODEL), BF16),
            pltpu.VMEM((tm, D_MODEL), F32),
        ],
        compiler_params=_params("parallel", "arbitrary"),
        name="moe",
    )(x, g, w_router, w_gate, w_up, w_down, g_final)


def kernel(x, mix_norm, ffn_norm, final_norm, w_in, ssm_a_re, ssm_a_im, ssm_log_dt, ssm_b_re, ssm_b_im,
           ssm_c_re, ssm_c_im, ssm_d, w_glu, p_attn, p_ssm, w_out, ffn_w_gate, ffn_w_up, ffn_w_down,
           w_router, moe_w_gate, moe_w_up, moe_w_down):
    assert x.shape == (1, SEQ, D_MODEL) and DEPTH == 2
    xs = x.reshape(SEQ, D_MODEL)
    col_scale = jnp.concatenate([jnp.full((SB_WIDTH,), HEAD_DIM ** -0.5 * LOG2E, F32),
                                 jnp.ones((IN_COLS - SB_WIDTH,), F32)])
    for layer in range(DEPTH):
        w_in_b = (w_in[layer] * col_scale).astype(BF16)
        proj, u, vt = _inproj(xs, mix_norm[layer][None], w_in_b)
        o_attn = _attention(proj, vt)
        ssm_w = _ssm_weights(ssm_a_re[layer], ssm_a_im[layer], ssm_log_dt[layer], ssm_b_re[layer],
                             ssm_b_im[layer], ssm_c_re[layer], ssm_c_im[layer], ssm_d[layer])
        y = _ssm(u, *ssm_w)
        xs = _post(xs, o_attn, y, proj, w_glu[layer].astype(BF16), p_attn[layer].astype(BF16),
                   p_ssm[layer].astype(BF16), w_out[layer].astype(BF16))
        i = layer // 2
        if layer % 2 == 0:
            xs = _ffn(xs, ffn_norm[layer][None], ffn_w_gate[i].astype(BF16), ffn_w_up[i].astype(BF16),
                      ffn_w_down[i].astype(BF16))
        else:
            wr = jnp.pad(w_router[i], ((0, 0), (0, LANES - N_EXPERTS)))
            xs = _moe(xs, ffn_norm[layer][None], wr, moe_w_gate[i].astype(BF16), moe_w_up[i].astype(BF16),
                      moe_w_down[i].astype(BF16), final_norm[None])
    return xs.reshape(1, SEQ, D_MODEL)
```

```python
import functools

import jax
import jax.numpy as jnp
from jax import lax
from jax.experimental import pallas as pl
from jax.experimental.pallas import tpu as pltpu

F32 = jnp.float32
BF16 = jnp.bfloat16

D_MODEL = 2048
SEQ = 16384
DEPTH = 2
HEAD_DIM = 128
HEADS = 8
SB_WIDTH = HEADS * HEAD_DIM
SSM_WIDTH = 1024
SSM_GROUP = 16
SSM_GROUPS = 64
SSM_STATE = 64
IN_COLS = 3 * SB_WIDTH + SSM_WIDTH + 2 * D_MODEL
D_FF_DENSE = 2 * D_MODEL
N_EXPERTS = 8
D_FF_EXPERT = D_MODEL // 2
RMS_EPS = 1e-6

LANES = 128
SUBLANES = 8
VMEM_LIMIT = 56 * 1024 * 1024

TQ = 256
TK = 256
HP = 4
LOG2E = 1.4426950408889634
UNDERFLOW_LOG2 = 152.0
NORM_MARGIN = 1.01
CHUNK = 8
N_CHUNKS = SEQ // CHUNK
SLAB = LANES
N_SLABS = SSM_WIDTH // SLAB
SLAB_GROUPS = SLAB // SSM_GROUP
SLAB_STATE = SLAB_GROUPS * SSM_STATE
SSM_ROWS = 256


def _params(*sem):
    return pltpu.CompilerParams(dimension_semantics=sem, vmem_limit_bytes=VMEM_LIMIT)


def _rms_rows(x, g):
    ms = jnp.mean(x * x, axis=-1, keepdims=True)
    return x * lax.rsqrt(ms + RMS_EPS) * g


def _sigmoid(x):
    return 1.0 / (1.0 + jnp.exp(-x))


def _inproj_kernel(x_ref, g_ref, w_ref, p_ref, u_ref, vt_ref, h_ref, s_ref):
    j = pl.program_id(1)

    @pl.when(j == 0)
    def _():
        h_ref[...] = _rms_rows(x_ref[...], g_ref[...]).astype(BF16)

    r = jnp.dot(h_ref[...], w_ref[...], preferred_element_type=F32)
    p_ref[...] = r.astype(BF16)

    @pl.when(j == 1)
    def _():
        rt = r[:, :SB_WIDTH].T
        for h in range(HEADS):
            for b in range(vt_ref.shape[1]):
                vt_ref[h, b] = rt[h * HEAD_DIM:(h + 1) * HEAD_DIM, b * TK:(b + 1) * TK].astype(BF16)
        slabs = s_ref.shape[0]
        rows = s_ref.shape[1] // CHUNK
        for l in range(slabs):
            s_ref[l] = r[:, SB_WIDTH + l * LANES:SB_WIDTH + (l + 1) * LANES]
        u_ref[...] = jnp.concatenate(
            [s_ref[l, pl.ds(c, rows, stride=CHUNK), :] for c in range(CHUNK) for l in range(slabs)],
            axis=1).astype(BF16)


def _inproj(x, g, w):
    tm, tn = 512, 2048
    assert tn == 2 * SB_WIDTH == SB_WIDTH + SSM_WIDTH and tm % TK == 0 and tm % CHUNK == 0
    return pl.pallas_call(
        _inproj_kernel,
        grid=(SEQ // tm, IN_COLS // tn),
        in_specs=[
            pl.BlockSpec((tm, D_MODEL), lambda i, j: (i, 0)),
            pl.BlockSpec((1, D_MODEL), lambda i, j: (0, 0)),
            pl.BlockSpec((D_MODEL, tn), lambda i, j: (0, j)),
        ],
        out_specs=[
            pl.BlockSpec((tm, tn), lambda i, j: (i, j)),
            pl.BlockSpec((tm // CHUNK, CHUNK * SSM_WIDTH), lambda i, j: (i, 0)),
            pl.BlockSpec((HEADS, tm // TK, HEAD_DIM, TK), lambda i, j: (0, i, 0, 0)),
        ],
        out_shape=[
            jax.ShapeDtypeStruct((SEQ, IN_COLS), BF16),
            jax.ShapeDtypeStruct((N_CHUNKS, CHUNK * SSM_WIDTH), BF16),
            jax.ShapeDtypeStruct((HEADS, SEQ // TK, HEAD_DIM, TK), BF16),
        ],
        scratch_shapes=[pltpu.VMEM((tm, D_MODEL), BF16), pltpu.VMEM((SSM_WIDTH // LANES, tm, LANES), F32)],
        compiler_params=_params("parallel", "arbitrary"),
        name="inproj",
    )(x, g, w)


def _attn_kernel(q_ref, k_ref, vt_ref, o_ref, kmax_ref, carry_ref, acc_ref):
    i = pl.program_id(1)
    hs = lambda h: slice(h * HEAD_DIM, (h + 1) * HEAD_DIM)

    @pl.when(i == 0)
    def _():
        def body(j, run):
            ks = pl.multiple_of(j * TK, TK)
            new, rows = [], []
            for h in range(HP):
                kb = k_ref[pl.ds(ks, TK), hs(h)].astype(F32)
                n2 = jnp.max(jnp.sum(kb * kb, axis=1, keepdims=True), axis=0, keepdims=True)
                new.append(jnp.maximum(run[h], n2))
                rows.append(jnp.broadcast_to(jnp.sqrt(new[h]), (1, TQ)))
            kmax_ref[j] = jnp.concatenate(rows, axis=0)
            return tuple(new)

        lax.fori_loop(0, SEQ // TK, body, tuple(jnp.zeros((1, 1), F32) for _ in range(HP)))

    row = lax.broadcasted_iota(jnp.int32, (TK, TQ), 0)
    col = lax.broadcasted_iota(jnp.int32, (TK, TQ), 1)
    upper = jnp.where(col >= row, 1.0, 0.0).astype(BF16)
    causal = row < col

    def block(h, j, carry, acc, masked):
        ks = pl.multiple_of(j * TK, TK)
        kb = k_ref[pl.ds(ks, TK), hs(h)]
        zt = lax.dot_general(kb, q_ref[:, hs(h)], (((1,), (1,)), ((), ())),
                             preferred_element_type=F32)
        sp = jnp.maximum(zt, 0.0) + jnp.log(1.0 + jnp.exp2(-jnp.abs(zt))) * LOG2E
        if masked:
            sp = jnp.where(causal, sp, 0.0)
        rt = jnp.dot(upper, sp.astype(BF16), preferred_element_type=F32)
        w = jnp.exp2(zt - rt - carry)
        if masked:
            w = jnp.where(causal, w, 0.0)
        acc = acc + jnp.dot(vt_ref[h, j], w.astype(BF16), preferred_element_type=F32)
        return carry + rt[0:1, :], acc

    prev = jnp.maximum(i - 1, 0)
    no_prev = jnp.where(i == 0, 1e30, 0.0).astype(F32)
    qmax = []
    for h in range(HP):
        carry, acc = block(h, i, jnp.zeros((1, TQ), F32), jnp.zeros((HEAD_DIM, TQ), F32), True)
        carry, acc = block(h, prev, carry + no_prev, acc, False)
        carry_ref[h] = carry
        acc_ref[h] = acc
        qf = q_ref[:, hs(h)].astype(F32)
        n2 = jnp.max(jnp.sum(qf * qf, axis=1, keepdims=True), axis=0, keepdims=True)
        qmax.append(jnp.sqrt(n2) * NORM_MARGIN)

    def more_needed(j):
        km = kmax_ref[j]
        worst = None
        for h in range(HP):
            slack = carry_ref[h] - (UNDERFLOW_LOG2 + qmax[h] * km[h:h + 1, :])
            worst = slack if worst is None else jnp.minimum(worst, slack)
        return (jnp.min(worst) < 0.0).astype(jnp.int32)

    def cond(c):
        return jnp.logical_and(c[0] >= 0, c[1] > 0)

    def body(c):
        j = c[0]
        for h in range(HP):
            carry, acc = block(h, j, carry_ref[h], acc_ref[h], False)
            carry_ref[h] = carry
            acc_ref[h] = acc
        return j - 1, more_needed(jnp.maximum(j - 1, 0))

    lax.while_loop(cond, body, (i - 2, more_needed(jnp.maximum(i - 2, 0))))
    for h in range(HP):
        o_ref[:, hs(h)] = acc_ref[h].T.astype(o_ref.dtype)


def _attention(proj, vt):
    wide = HP * HEAD_DIM
    once = pl.Buffered(1)
    return pl.pallas_call(
        _attn_kernel,
        grid=(HEADS // HP, SEQ // TQ),
        in_specs=[
            pl.BlockSpec((TQ, wide), lambda g, i: (i, g)),
            pl.BlockSpec((SEQ, wide), lambda g, i: (0, HEADS // HP + g), pipeline_mode=once),
            pl.BlockSpec((HP, SEQ // TK, HEAD_DIM, TK), lambda g, i: (g, 0, 0, 0), pipeline_mode=once),
        ],
        out_specs=pl.BlockSpec((TQ, wide), lambda g, i: (i, g)),
        out_shape=jax.ShapeDtypeStruct((SEQ, SB_WIDTH), BF16),
        scratch_shapes=[
            pltpu.VMEM((SEQ // TK, HP, TQ), F32),
            pltpu.VMEM((HP, 1, TQ), F32),
            pltpu.VMEM((HP, HEAD_DIM, TQ), F32),
        ],
        compiler_params=_params("arbitrary", "arbitrary"),
        name="attention",
    )(proj, proj, vt)


def _ssm_weights(a_re, a_im, log_dt, b_re, b_im, c_re, c_im, d_skip):
    dt = jnp.exp(log_dt)[:, None]
    d = jnp.arange(CHUNK + 1, dtype=F32)[:, None, None]
    mag = jnp.exp(d * (dt * a_re))
    pw_re = mag * jnp.cos(d * (dt * a_im))
    pw_im = mag * jnp.sin(d * (dt * a_im))
    abar_re, abar_im = pw_re[1], pw_im[1]
    den = a_re * a_re + a_im * a_im
    num_re = abar_re - 1.0
    zoh_re = (num_re * a_re + abar_im * a_im) / den
    zoh_im = (abar_im * a_re - num_re * a_im) / den
    bb_re = zoh_re[..., None] * b_re - zoh_im[..., None] * b_im
    bb_im = zoh_re[..., None] * b_im + zoh_im[..., None] * b_re
    ca_re = c_re[None] * pw_re[:, :, None, :] - c_im[None] * pw_im[:, :, None, :]
    ca_im = c_re[None] * pw_im[:, :, None, :] + c_im[None] * pw_re[:, :, None, :]
    kd = jnp.einsum('dgon,gnp->dgop', ca_re, bb_re) - jnp.einsum('dgon,gnp->dgop', ca_im, bb_im)
    kd = kd.at[0].add(jnp.eye(SSM_GROUP, dtype=F32)[None] * d_skip[:, :, None])
    by_slab = lambda t: t.reshape((N_SLABS, SLAB_GROUPS) + t.shape[1:])
    k_rep = by_slab(kd[:CHUNK].transpose(1, 0, 3, 2)).transpose(0, 2, 3, 1, 4)
    k_rep = k_rep.reshape(N_SLABS, CHUNK, SSM_GROUP, SLAB)
    rev_re, rev_im = pw_re[CHUNK - 1::-1], pw_im[CHUNK - 1::-1]
    wz_re = rev_re[:, :, :, None] * bb_re[None] - rev_im[:, :, :, None] * bb_im[None]
    wz_im = rev_re[:, :, :, None] * bb_im[None] + rev_im[:, :, :, None] * bb_re[None]
    wz = jnp.stack([wz_re, wz_im], axis=0).transpose(2, 1, 4, 0, 3)
    wz_rep = by_slab(wz).transpose(0, 2, 3, 4, 1, 5).reshape(N_SLABS, CHUNK, SSM_GROUP, 2 * SLAB_STATE)
    v = jnp.stack([ca_re[1:], -ca_im[1:]], axis=0).transpose(2, 0, 4, 1, 3)
    v_rep = by_slab(v).transpose(0, 2, 3, 4, 1, 5).reshape(N_SLABS, 2, SSM_STATE, CHUNK * SLAB)
    lam = jnp.stack([pw_re[CHUNK], pw_im[CHUNK]], axis=1)
    lam = by_slab(lam).transpose(0, 2, 1, 3).reshape(N_SLABS, 2, SLAB_STATE)
    wz_mat, t_mat, v_mat = _ssm_expand(k_rep, wz_rep, v_rep)
    return wz_mat, t_mat, v_mat, lam


def _ssm_expand_kernel(k_ref, wz_ref, v_ref, wz_out, t_out, v_out):
    def own_group(rows, log2_per_row, cols, col_period, log2_per_col):
        a = lax.broadcasted_iota(jnp.int32, (rows, cols), 0) >> log2_per_row
        c = (lax.broadcasted_iota(jnp.int32, (rows, cols), 1) & (col_period - 1)) >> log2_per_col
        return a == c

    def expand(vals, mask):
        return jnp.where(mask, jnp.tile(vals, (SLAB_GROUPS, 1)), 0.0).astype(BF16)

    m_t = own_group(SLAB, 4, SLAB, SLAB, 4)
    lag_blocks = [expand(k_ref[d], m_t) for d in range(CHUNK)]
    for j in range(CHUNK):
        for b in range(CHUNK):
            blk = lag_blocks[b - j] if b >= j else jnp.zeros((SLAB, SLAB), BF16)
            t_out[j * SLAB:(j + 1) * SLAB, b * SLAB:(b + 1) * SLAB] = blk
    m_w = own_group(SLAB, 4, 2 * SLAB_STATE, SLAB_STATE, 6)
    for j in range(CHUNK):
        wz_out[j * SLAB:(j + 1) * SLAB, :] = expand(wz_ref[j], m_w)
    m_v = own_group(SLAB_STATE, 6, CHUNK * SLAB, SLAB, 4)
    for r in range(2):
        v_out[r * SLAB_STATE:(r + 1) * SLAB_STATE, :] = expand(v_ref[r], m_v)


def _ssm_expand(k_rep, wz_rep, v_rep):
    assert SSM_GROUP == 1 << 4 and SSM_STATE == 1 << 6
    mat = jax.ShapeDtypeStruct((N_SLABS, CHUNK * SLAB, CHUNK * SLAB), BF16)
    slab3 = lambda shape: pl.BlockSpec((None,) + shape, lambda k: (k, 0, 0))
    return pl.pallas_call(
        _ssm_expand_kernel,
        grid=(N_SLABS,),
        in_specs=[
            pl.BlockSpec((None, CHUNK, SSM_GROUP, SLAB), lambda k: (k, 0, 0, 0)),
            pl.BlockSpec((None, CHUNK, SSM_GROUP, 2 * SLAB_STATE), lambda k: (k, 0, 0, 0)),
            pl.BlockSpec((None, 2, SSM_STATE, CHUNK * SLAB), lambda k: (k, 0, 0, 0)),
        ],
        out_specs=[slab3((CHUNK * SLAB, 2 * SLAB_STATE)), slab3((CHUNK * SLAB, CHUNK * SLAB)),
                   slab3((2 * SLAB_STATE, CHUNK * SLAB))],
        out_shape=[mat, mat, mat],
        compiler_params=_params("parallel"),
        name="ssm_expand",
    )(k_rep, wz_rep, v_rep)


def _cmul(ar, ai, br, bi):
    return ar * br - ai * bi, ar * bi + ai * br


def _ssm_kernel(*refs):
    u_refs = refs[:CHUNK]
    wz_ref, t_ref, v_ref, lam_ref, o_ref, s_ref = refs[CHUNK:]
    ns = SLAB_STATE
    lam1 = (lam_ref[0:1, :], lam_ref[1:2, :])
    lam2 = _cmul(*lam1, *lam1)
    lam4 = _cmul(*lam2, *lam2)
    lam8 = _cmul(*lam4, *lam4)

    def lhs_rows(r0, n):
        return jnp.concatenate([u[pl.ds(r0, n), :] for u in u_refs], axis=1)

    def pass1(b, _):
        r0 = pl.multiple_of(b * SSM_ROWS, SSM_ROWS)
        z = jnp.dot(lhs_rows(r0, SSM_ROWS), wz_ref[...], preferred_element_type=F32)
        zr, zi = z[:, :ns], z[:, ns:]
        sub = lax.broadcasted_iota(jnp.int32, (SSM_ROWS, ns), 0) & (SUBLANES - 1)
        for shift, (cr, ci) in ((1, lam1), (2, lam2), (4, lam4)):
            keep = sub >= shift
            pr = jnp.where(keep, pltpu.roll(zr, shift, axis=0), 0.0)
            pi = jnp.where(keep, pltpu.roll(zi, shift, axis=0), 0.0)
            ar, ai = _cmul(cr, ci, pr, pi)
            zr, zi = zr + ar, zi + ai
        s_ref[pl.ds(r0, SSM_ROWS), :ns] = zr
        s_ref[pl.ds(r0, SSM_ROWS), ns:] = zi
        return 0

    lax.fori_loop(0, N_CHUNKS // SSM_ROWS, pass1, 0, unroll=2)

    pow_re, pow_im = [jnp.ones((1, ns), F32)], [jnp.zeros((1, ns), F32)]
    for _ in range(SUBLANES - 1):
        nr, ni = _cmul(pow_re[-1], pow_im[-1], *lam1)
        pow_re.append(nr)
        pow_im.append(ni)
    tab_re = jnp.concatenate(pow_re, axis=0)
    tab_im = jnp.concatenate(pow_im, axis=0)
    sub8 = lax.broadcasted_iota(jnp.int32, (SUBLANES, ns), 0)

    def pass2(t, c):
        cr, ci = c
        r0 = pl.multiple_of(t * SUBLANES, SUBLANES)
        lr = s_ref[pl.ds(r0, SUBLANES), :ns]
        li = s_ref[pl.ds(r0, SUBLANES), ns:]
        er = jnp.where(sub8 >= 1, pltpu.roll(lr, 1, axis=0), 0.0)
        ei = jnp.where(sub8 >= 1, pltpu.roll(li, 1, axis=0), 0.0)
        ar, ai = _cmul(tab_re, tab_im, cr, ci)
        s_ref[pl.ds(r0, SUBLANES), :ns] = er + ar
        s_ref[pl.ds(r0, SUBLANES), ns:] = ei + ai
        nr, ni = _cmul(*lam8, cr, ci)
        return lr[SUBLANES - 1:SUBLANES, :] + nr, li[SUBLANES - 1:SUBLANES, :] + ni

    zero = jnp.zeros((1, ns), F32)
    lax.fori_loop(0, N_CHUNKS // SUBLANES, pass2, (zero, zero), unroll=4)

    for b in range(N_CHUNKS // SSM_ROWS):
        r0 = b * SSM_ROWS
        y = jnp.dot(lhs_rows(r0, SSM_ROWS), t_ref[...], preferred_element_type=F32)
        y = y + jnp.dot(s_ref[pl.ds(r0, SSM_ROWS), :].astype(BF16), v_ref[...],
                        preferred_element_type=F32)
        for j in range(CHUNK):
            o_ref[pl.ds(r0 * CHUNK + j, SSM_ROWS, stride=CHUNK), :] = y[:, j * SLAB:(j + 1) * SLAB]


def _ssm(u2, wz_mat, t_mat, v_mat, lam):
    u_specs = [pl.BlockSpec((N_CHUNKS, SLAB), functools.partial(lambda j, k: (0, j * N_SLABS + k), j))
               for j in range(CHUNK)]
    return pl.pallas_call(
        _ssm_kernel,
        grid=(N_SLABS,),
        in_specs=u_specs + [
            pl.BlockSpec((None, CHUNK * SLAB, 2 * SLAB_STATE), lambda k: (k, 0, 0)),
            pl.BlockSpec((None, CHUNK * SLAB, CHUNK * SLAB), lambda k: (k, 0, 0)),
            pl.BlockSpec((None, 2 * SLAB_STATE, CHUNK * SLAB), lambda k: (k, 0, 0)),
            pl.BlockSpec((None, 2, SLAB_STATE), lambda k: (k, 0, 0)),
        ],
        out_specs=pl.BlockSpec((SEQ, SLAB), lambda k: (0, k), pipeline_mode=pl.Buffered(1)),
        out_shape=jax.ShapeDtypeStruct((SEQ, SSM_WIDTH), F32),
        scratch_shapes=[pltpu.VMEM((N_CHUNKS, 2 * SLAB_STATE), F32)],
        compiler_params=_params("arbitrary"),
        name="ssm",
    )(*([u2] * CHUNK), wz_mat, t_mat, v_mat, lam)


def _gelu_tanh(x):
    c = 0.7978845608028654
    return 0.5 * x * (1.0 + jnp.tanh(c * (x + 0.044715 * (x * x * x))))


def _post_kernel(x_ref, oa_ref, y_ref, ga_ref, gb_ref, wglu_ref, pa_ref, ps_ref, wo_ref, o_ref):
    y = _gelu_tanh(y_ref[...].astype(F32))
    yb = y.astype(BF16)
    glu = jnp.dot(yb, wglu_ref[...], preferred_element_type=F32)
    o_ssm = (y * _sigmoid(glu)).astype(BF16)
    a = jnp.dot(oa_ref[...], pa_ref[...], preferred_element_type=F32)
    b = jnp.dot(o_ssm, ps_ref[...], preferred_element_type=F32)
    merged = _sigmoid(ga_ref[...].astype(F32)) * a + _sigmoid(gb_ref[...].astype(F32)) * b
    o_ref[...] = x_ref[...] + jnp.dot(merged.astype(BF16), wo_ref[...], preferred_element_type=F32)


def _post(x, o_attn, y, proj, w_glu, p_attn, p_ssm, w_out):
    tm = 256
    ga_blk = (3 * SB_WIDTH + SSM_WIDTH) // D_MODEL
    const = lambda shape: pl.BlockSpec(shape, lambda i: (0, 0), pipeline_mode=pl.Buffered(1))
    return pl.pallas_call(
        _post_kernel,
        grid=(SEQ // tm,),
        in_specs=[
            pl.BlockSpec((tm, D_MODEL), lambda i: (i, 0)),
            pl.BlockSpec((tm, SB_WIDTH), lambda i: (i, 0)),
            pl.BlockSpec((tm, SSM_WIDTH), lambda i: (i, 0)),
            pl.BlockSpec((tm, D_MODEL), lambda i: (i, ga_blk)),
            pl.BlockSpec((tm, D_MODEL), lambda i: (i, ga_blk + 1)),
            const((SSM_WIDTH, SSM_WIDTH)),
            const((SB_WIDTH, D_MODEL)),
            const((SSM_WIDTH, D_MODEL)),
            const((D_MODEL, D_MODEL)),
        ],
        out_specs=pl.BlockSpec((tm, D_MODEL), lambda i: (i, 0)),
        out_shape=jax.ShapeDtypeStruct((SEQ, D_MODEL), F32),
        compiler_params=_params("parallel"),
        name="post",
    )(x, o_attn, y, proj, proj, w_glu, p_attn, p_ssm, w_out)


def _ffn_kernel(x_ref, g_ref, wg_ref, wu_ref, wd_ref, o_ref, h_ref):
    f = pl.program_id(1)

    @pl.when(f == 0)
    def _():
        x = x_ref[...]
        h_ref[...] = _rms_rows(x, g_ref[...]).astype(BF16)
        o_ref[...] = x

    h = h_ref[...]
    a = jnp.dot(h, wg_ref[...], preferred_element_type=F32)
    b = jnp.dot(h, wu_ref[...], preferred_element_type=F32)
    t = (a * _sigmoid(a) * b).astype(BF16)
    o_ref[...] += jnp.dot(t, wd_ref[...], preferred_element_type=F32)


def _ffn(x, g, w_gate, w_up, w_down):
    tm, tf = 512, 1024
    return pl.pallas_call(
        _ffn_kernel,
        grid=(SEQ // tm, D_FF_DENSE // tf),
        in_specs=[
            pl.BlockSpec((tm, D_MODEL), lambda i, f: (i, 0)),
            pl.BlockSpec((1, D_MODEL), lambda i, f: (0, 0)),
            pl.BlockSpec((D_MODEL, tf), lambda i, f: (0, f)),
            pl.BlockSpec((D_MODEL, tf), lambda i, f: (0, f)),
            pl.BlockSpec((tf, D_MODEL), lambda i, f: (f, 0)),
        ],
        out_specs=pl.BlockSpec((tm, D_MODEL), lambda i, f: (i, 0)),
        out_shape=jax.ShapeDtypeStruct((SEQ, D_MODEL), F32),
        scratch_shapes=[pltpu.VMEM((tm, D_MODEL), BF16)],
        compiler_params=_params("parallel", "arbitrary"),
        name="ffn",
    )(x, g, w_gate, w_up, w_down)


MOE_TM = 512
MOE_SUB = 2
MOE_ROWS = 160
MOE_PAD = 256


def _moe_kernel(x_ref, g_ref, wr_ref, wg_ref, wu_ref, wd_ref, gf_ref, o_ref,
                h_ref, rw_ref, rk_ref, rkt_ref, cnt_ref, y_ref):
    e = pl.program_id(1)
    tm = MOE_TM
    lane = lax.broadcasted_iota(jnp.int32, (tm, LANES), 1)
    tiles = [slice(u * tm, (u + 1) * tm) for u in range(MOE_SUB)]

    @pl.when(e == 0)
    def _():
        before = (lax.broadcasted_iota(jnp.int32, (tm, tm), 1)
                  < lax.broadcasted_iota(jnp.int32, (tm, tm), 0))
        before = jnp.where(before, 1.0, 0.0).astype(BF16)
        for u, rows in enumerate(tiles):
            x = x_ref[rows, :]
            hf = _rms_rows(x, g_ref[...])
            h_ref[rows, :] = hf.astype(BF16)
            o_ref[rows, :] = x
            logits = jnp.dot(hf, wr_ref[...], preferred_element_type=F32, precision=lax.Precision.HIGHEST)
            lg = jnp.where(lane < N_EXPERTS, logits, -jnp.inf)
            m1 = jnp.max(lg, axis=-1, keepdims=True)
            i1 = jnp.min(jnp.where(lg == m1, lane, LANES), axis=-1, keepdims=True)
            lg2 = jnp.where(lane == i1, -jnp.inf, lg)
            m2 = jnp.max(lg2, axis=-1, keepdims=True)
            i2 = jnp.min(jnp.where(lg2 == m2, lane, LANES), axis=-1, keepdims=True)
            e2 = jnp.exp(m2 - m1)
            g1 = 1.0 / (1.0 + e2)
            g2 = e2 / (1.0 + e2)
            rw_ref[rows, :] = jnp.where(lane == i1, g1, 0.0) + jnp.where(lane == i2, g2, 0.0)
            routed = jnp.logical_or(lane == i1, lane == i2)
            onehot = jnp.where(routed, 1.0, 0.0)
            ahead = jnp.dot(before, onehot.astype(BF16), preferred_element_type=F32)
            rk = jnp.where(routed, ahead, -1.0)
            rk_ref[rows, :] = rk
            rkt_ref[u] = rk.T
            cnt_ref[u] = jnp.sum(onehot, axis=0, keepdims=True)
        y_ref[...] = jnp.zeros_like(y_ref)

    lane1 = lax.broadcasted_iota(jnp.int32, (1, LANES), 1)
    row_id = lax.broadcasted_iota(jnp.int32, (MOE_ROWS, tm), 0).astype(F32)
    col_id = lax.broadcasted_iota(jnp.int32, (tm, MOE_PAD), 1).astype(F32)

    def route_tile(u, rows):
        pick = lambda ref: jnp.sum(jnp.where(lane == e, ref[rows, :], 0.0), axis=-1, keepdims=True)
        rk_col, gate_col = pick(rk_ref), pick(rw_ref)
        rk_row = rkt_ref[u, pl.ds(e, 1), :]
        count = jnp.sum(jnp.where(lane1 == e, cnt_ref[u], 0.0)).astype(jnp.int32)

        def one_pass(s, _):
            base = (s * MOE_ROWS).astype(F32)
            sel = jnp.where(rk_row - base == row_id, 1.0, 0.0).astype(BF16)
            xc = jnp.dot(sel, h_ref[rows, :], preferred_element_type=F32).astype(BF16)
            a = jnp.dot(xc, wg_ref[...], preferred_element_type=F32)
            b = jnp.dot(xc, wu_ref[...], preferred_element_type=F32)
            t = (a * _sigmoid(a) * b).astype(BF16)
            y_ref[:MOE_ROWS, :] = jnp.dot(t, wd_ref[...], preferred_element_type=F32).astype(BF16)
            back = jnp.where(rk_col - base == col_id, 1.0, 0.0).astype(BF16)
            o_ref[rows, :] += gate_col * jnp.dot(back, y_ref[...], preferred_element_type=F32)
            return 0

        lax.fori_loop(0, (count + MOE_ROWS - 1) // MOE_ROWS, one_pass, 0)

    for u, rows in enumerate(tiles):
        route_tile(u, rows)

    @pl.when(e == pl.num_programs(1) - 1)
    def _():
        for rows in tiles:
            o_ref[rows, :] = _rms_rows(o_ref[rows, :], gf_ref[...])


def _moe(x, g, w_router, w_gate, w_up, w_down, g_final):
    tm = MOE_TM * MOE_SUB
    once = pl.Buffered(1)
    return pl.pallas_call(
        _moe_kernel,
        grid=(SEQ // tm, N_EXPERTS),
        in_specs=[
            pl.BlockSpec((tm, D_MODEL), lambda i, e: (i, 0), pipeline_mode=once),
            pl.BlockSpec((1, D_MODEL), lambda i, e: (0, 0)),
            pl.BlockSpec((D_MODEL, LANES), lambda i, e: (0, 0)),
            pl.BlockSpec((None, D_MODEL, D_FF_EXPERT), lambda i, e: (e, 0, 0)),
            pl.BlockSpec((None, D_MODEL, D_FF_EXPERT), lambda i, e: (e, 0, 0)),
            pl.BlockSpec((None, D_FF_EXPERT, D_MODEL), lambda i, e: (e, 0, 0)),
            pl.BlockSpec((1, D_MODEL), lambda i, e: (0, 0)),
        ],
        out_specs=pl.BlockSpec((tm, D_MODEL), lambda i, e: (i, 0), pipeline_mode=once),
        out_shape=jax.ShapeDtypeStruct((SEQ, D_MODEL), F32),
        scratch_shapes=[
            pltpu.VMEM((tm, D_MODEL), BF16),
            pltpu.VMEM((tm, LANES), F32),
            pltpu.VMEM((tm, LANES), F32),
            pltpu.VMEM((MOE_SUB, LANES, MOE_TM), F32),
            pltpu.VMEM((MOE_SUB, 1, LANES), F32),
            pltpu.VMEM((MOE_PAD, D_MODEL), BF16),
        ],
        compiler_params=_params("parallel", "arbitrary"),
        name="moe",
    )(x, g, w_router, w_gate, w_up, w_down, g_final)


def kernel(x, mix_norm, ffn_norm, final_norm, w_in, ssm_a_re, ssm_a_im, ssm_log_dt, ssm_b_re, ssm_b_im,
           ssm_c_re, ssm_c_im, ssm_d, w_glu, p_attn, p_ssm, w_out, ffn_w_gate, ffn_w_up, ffn_w_down,
           w_router, moe_w_gate, moe_w_up, moe_w_down):
    assert x.shape == (1, SEQ, D_MODEL) and DEPTH == 2
    xs = x.reshape(SEQ, D_MODEL)
    col_scale = jnp.concatenate([jnp.full((SB_WIDTH,), HEAD_DIM ** -0.5 * LOG2E, F32),
                                 jnp.ones((IN_COLS - SB_WIDTH,), F32)])
    for layer in range(DEPTH):
        w_in_b = (w_in[layer] * col_scale).astype(BF16)
        proj, u, vt = _inproj(xs, mix_norm[layer][None], w_in_b)
        o_attn = _attention(proj, vt)
        ssm_w = _ssm_weights(ssm_a_re[layer], ssm_a_im[layer], ssm_log_dt[layer], ssm_b_re[layer],
                             ssm_b_im[layer], ssm_c_re[layer], ssm_c_im[layer], ssm_d[layer])
        y = _ssm(u, *ssm_w)
        xs = _post(xs, o_attn, y, proj, w_glu[layer].astype(BF16), p_attn[layer].astype(BF16),
                   p_ssm[layer].astype(BF16), w_out[layer].astype(BF16))
        i = layer // 2
        if layer % 2 == 0:
            xs = _ffn(xs, ffn_norm[layer][None], ffn_w_gate[i].astype(BF16), ffn_w_up[i].astype(BF16),
                      ffn_w_down[i].astype(BF16))
        else:
            wr = jnp.pad(w_router[i], ((0, 0), (0, LANES - N_EXPERTS)))
            xs = _moe(xs, ffn_norm[layer][None], wr, moe_w_gate[i].astype(BF16), moe_w_up[i].astype(BF16),
                      moe_w_down[i].astype(BF16), final_norm[None])
    return xs.reshape(1, SEQ, D_MODEL)
```

```python
import functools

import jax
import jax.numpy as jnp
from jax import lax
from jax.experimental import pallas as pl
from jax.experimental.pallas import tpu as pltpu

F32 = jnp.float32
BF16 = jnp.bfloat16

D_MODEL = 2048
SEQ = 16384
DEPTH = 2
HEAD_DIM = 128
HEADS = 8
SB_WIDTH = HEADS * HEAD_DIM
SSM_WIDTH = 1024
SSM_GROUP = 16
SSM_GROUPS = 64
SSM_STATE = 64
IN_COLS = 3 * SB_WIDTH + SSM_WIDTH + 2 * D_MODEL
D_FF_DENSE = 2 * D_MODEL
N_EXPERTS = 8
D_FF_EXPERT = D_MODEL // 2
RMS_EPS = 1e-6

LANES = 128
SUBLANES = 8
VMEM_LIMIT = 56 * 1024 * 1024

TQ = 256
TK = 256
HP = 4
LOG2E = 1.4426950408889634
UNDERFLOW_LOG2 = 152.0
NORM_MARGIN = 1.01
CHUNK = 8
N_CHUNKS = SEQ // CHUNK
SLAB = LANES
N_SLABS = SSM_WIDTH // SLAB
SLAB_GROUPS = SLAB // SSM_GROUP
SLAB_STATE = SLAB_GROUPS * SSM_STATE
SSM_ROWS = 256


def _params(*sem):
    return pltpu.CompilerParams(dimension_semantics=sem, vmem_limit_bytes=VMEM_LIMIT)


def _rms_rows(x, g):
    ms = jnp.mean(x * x, axis=-1, keepdims=True)
    return x * lax.rsqrt(ms + RMS_EPS) * g


def _sigmoid(x):
    return 1.0 / (1.0 + jnp.exp(-x))


def _inproj_kernel(x_ref, g_ref, w_ref, p_ref, u_ref, vt_ref, h_ref, s_ref):
    j = pl.program_id(1)

    @pl.when(j == 0)
    def _():
        h_ref[...] = _rms_rows(x_ref[...], g_ref[...]).astype(BF16)

    r = jnp.dot(h_ref[...], w_ref[...], preferred_element_type=F32)
    p_ref[...] = r.astype(BF16)

    @pl.when(j == 1)
    def _():
        rt = r[:, :SB_WIDTH].T
        for h in range(HEADS):
            for b in range(vt_ref.shape[1]):
                vt_ref[h, b] = rt[h * HEAD_DIM:(h + 1) * HEAD_DIM, b * TK:(b + 1) * TK].astype(BF16)
        slabs = s_ref.shape[0]
        rows = s_ref.shape[1] // CHUNK
        for l in range(slabs):
            s_ref[l] = r[:, SB_WIDTH + l * LANES:SB_WIDTH + (l + 1) * LANES]
        u_ref[...] = jnp.concatenate(
            [s_ref[l, pl.ds(c, rows, stride=CHUNK), :] for c in range(CHUNK) for l in range(slabs)],
            axis=1).astype(BF16)


def _inproj(x, g, w):
    tm, tn = 512, 2048
    assert tn == 2 * SB_WIDTH == SB_WIDTH + SSM_WIDTH and tm % TK == 0 and tm % CHUNK == 0
    return pl.pallas_call(
        _inproj_kernel,
        grid=(SEQ // tm, IN_COLS // tn),
        in_specs=[
            pl.BlockSpec((tm, D_MODEL), lambda i, j: (i, 0)),
            pl.BlockSpec((1, D_MODEL), lambda i, j: (0, 0)),
            pl.BlockSpec((D_MODEL, tn), lambda i, j: (0, j)),
        ],
        out_specs=[
            pl.BlockSpec((tm, tn), lambda i, j: (i, j)),
            pl.BlockSpec((tm // CHUNK, CHUNK * SSM_WIDTH), lambda i, j: (i, 0)),
            pl.BlockSpec((HEADS, tm // TK, HEAD_DIM, TK), lambda i, j: (0, i, 0, 0)),
        ],
        out_shape=[
            jax.ShapeDtypeStruct((SEQ, IN_COLS), BF16),
            jax.ShapeDtypeStruct((N_CHUNKS, CHUNK * SSM_WIDTH), BF16),
            jax.ShapeDtypeStruct((HEADS, SEQ // TK, HEAD_DIM, TK), BF16),
        ],
        scratch_shapes=[pltpu.VMEM((tm, D_MODEL), BF16), pltpu.VMEM((SSM_WIDTH // LANES, tm, LANES), F32)],
        compiler_params=_params("parallel", "arbitrary"),
        name="inproj",
    )(x, g, w)


def _attn_kernel(q_ref, k_ref, vt_ref, o_ref, kmax_ref, carry_ref, acc_ref):
    i = pl.program_id(1)
    hs = lambda h: slice(h * HEAD_DIM, (h + 1) * HEAD_DIM)
    ones = jnp.ones((HEAD_DIM, LANES), BF16)

    def row_norm2(rows_bf16):
        sq = rows_bf16.astype(F32)
        return jnp.dot((sq * sq).astype(BF16), ones, preferred_element_type=F32)

    @pl.when(i == 0)
    def _():
        def body(j, run):
            ks = pl.multiple_of(j * TK, TK)
            new, rows = [], []
            for h in range(HP):
                n2 = jnp.max(row_norm2(k_ref[pl.ds(ks, TK), hs(h)]), axis=0, keepdims=True)[:, :1]
                new.append(jnp.maximum(run[h], n2))
                rows.append(jnp.broadcast_to(jnp.sqrt(new[h]), (1, TQ)))
            kmax_ref[j] = jnp.concatenate(rows, axis=0)
            return tuple(new)

        lax.fori_loop(0, SEQ // TK, body, tuple(jnp.zeros((1, 1), F32) for _ in range(HP)), unroll=4)

    row = lax.broadcasted_iota(jnp.int32, (TK, TQ), 0)
    col = lax.broadcasted_iota(jnp.int32, (TK, TQ), 1)
    upper = jnp.where(col >= row, 1.0, 0.0).astype(BF16)
    causal = row < col

    def block(h, j, carry, acc, masked):
        ks = pl.multiple_of(j * TK, TK)
        kb = k_ref[pl.ds(ks, TK), hs(h)]
        zt = lax.dot_general(kb, q_ref[:, hs(h)], (((1,), (1,)), ((), ())),
                             preferred_element_type=F32)
        sp = jnp.maximum(zt, 0.0) + jnp.log(1.0 + jnp.exp2(-jnp.abs(zt))) * LOG2E
        if masked:
            sp = jnp.where(causal, sp, 0.0)
        rt = jnp.dot(upper, sp.astype(BF16), preferred_element_type=F32)
        w = jnp.exp2(zt - rt - carry)
        if masked:
            w = jnp.where(causal, w, 0.0)
        acc = acc + jnp.dot(vt_ref[h, j], w.astype(BF16), preferred_element_type=F32)
        return carry + rt[0:1, :], acc

    prev = jnp.maximum(i - 1, 0)
    no_prev = jnp.where(i == 0, 1e30, 0.0).astype(F32)
    qmax = []
    for h in range(HP):
        carry, acc = block(h, i, jnp.zeros((1, TQ), F32), jnp.zeros((HEAD_DIM, TQ), F32), True)
        carry, acc = block(h, prev, carry + no_prev, acc, False)
        carry_ref[h] = carry
        acc_ref[h] = acc
        qf = q_ref[:, hs(h)].astype(F32)
        n2 = jnp.max(jnp.sum(qf * qf, axis=1, keepdims=True), axis=0, keepdims=True)
        qmax.append(jnp.sqrt(n2) * NORM_MARGIN)

    def more_needed(j):
        km = kmax_ref[j]
        worst = None
        for h in range(HP):
            slack = carry_ref[h] - (UNDERFLOW_LOG2 + qmax[h] * km[h:h + 1, :])
            worst = slack if worst is None else jnp.minimum(worst, slack)
        return (jnp.min(worst) < 0.0).astype(jnp.int32)

    def cond(c):
        return jnp.logical_and(c[0] >= 0, c[1] > 0)

    def body(c):
        j = c[0]
        for h in range(HP):
            carry, acc = block(h, j, carry_ref[h], acc_ref[h], False)
            carry_ref[h] = carry
            acc_ref[h] = acc
        return j - 1, more_needed(jnp.maximum(j - 1, 0))

    lax.while_loop(cond, body, (i - 2, more_needed(jnp.maximum(i - 2, 0))))
    for h in range(HP):
        o_ref[:, hs(h)] = acc_ref[h].T.astype(o_ref.dtype)


def _attention(proj, vt):
    wide = HP * HEAD_DIM
    once = pl.Buffered(1)
    return pl.pallas_call(
        _attn_kernel,
        grid=(HEADS // HP, SEQ // TQ),
        in_specs=[
            pl.BlockSpec((TQ, wide), lambda g, i: (i, g)),
            pl.BlockSpec((SEQ, wide), lambda g, i: (0, HEADS // HP + g), pipeline_mode=once),
            pl.BlockSpec((HP, SEQ // TK, HEAD_DIM, TK), lambda g, i: (g, 0, 0, 0), pipeline_mode=once),
        ],
        out_specs=pl.BlockSpec((TQ, wide), lambda g, i: (i, g)),
        out_shape=jax.ShapeDtypeStruct((SEQ, SB_WIDTH), BF16),
        scratch_shapes=[
            pltpu.VMEM((SEQ // TK, HP, TQ), F32),
            pltpu.VMEM((HP, 1, TQ), F32),
            pltpu.VMEM((HP, HEAD_DIM, TQ), F32),
        ],
        compiler_params=_params("arbitrary", "arbitrary"),
        name="attention",
    )(proj, proj, vt)


def _ssm_weights(a_re, a_im, log_dt, b_re, b_im, c_re, c_im, d_skip):
    dt = jnp.exp(log_dt)[:, None]
    d = jnp.arange(CHUNK + 1, dtype=F32)[:, None, None]
    mag = jnp.exp(d * (dt * a_re))
    pw_re = mag * jnp.cos(d * (dt * a_im))
    pw_im = mag * jnp.sin(d * (dt * a_im))
    abar_re, abar_im = pw_re[1], pw_im[1]
    den = a_re * a_re + a_im * a_im
    num_re = abar_re - 1.0
    zoh_re = (num_re * a_re + abar_im * a_im) / den
    zoh_im = (abar_im * a_re - num_re * a_im) / den
    bb_re = zoh_re[..., None] * b_re - zoh_im[..., None] * b_im
    bb_im = zoh_re[..., None] * b_im + zoh_im[..., None] * b_re
    ca_re = c_re[None] * pw_re[:, :, None, :] - c_im[None] * pw_im[:, :, None, :]
    ca_im = c_re[None] * pw_im[:, :, None, :] + c_im[None] * pw_re[:, :, None, :]
    kd = jnp.einsum('dgon,gnp->dgop', ca_re, bb_re) - jnp.einsum('dgon,gnp->dgop', ca_im, bb_im)
    kd = kd.at[0].add(jnp.eye(SSM_GROUP, dtype=F32)[None] * d_skip[:, :, None])
    by_slab = lambda t: t.reshape((N_SLABS, SLAB_GROUPS) + t.shape[1:])
    k_rep = by_slab(kd[:CHUNK].transpose(1, 0, 3, 2)).transpose(0, 2, 3, 1, 4)
    k_rep = k_rep.reshape(N_SLABS, CHUNK, SSM_GROUP, SLAB)
    rev_re, rev_im = pw_re[CHUNK - 1::-1], pw_im[CHUNK - 1::-1]
    wz_re = rev_re[:, :, :, None] * bb_re[None] - rev_im[:, :, :, None] * bb_im[None]
    wz_im = rev_re[:, :, :, None] * bb_im[None] + rev_im[:, :, :, None] * bb_re[None]
    wz = jnp.stack([wz_re, wz_im], axis=0).transpose(2, 1, 4, 0, 3)
    wz_rep = by_slab(wz).transpose(0, 2, 3, 4, 1, 5).reshape(N_SLABS, CHUNK, SSM_GROUP, 2 * SLAB_STATE)
    v = jnp.stack([ca_re[1:], -ca_im[1:]], axis=0).transpose(2, 0, 4, 1, 3)
    v_rep = by_slab(v).transpose(0, 2, 3, 4, 1, 5).reshape(N_SLABS, 2, SSM_STATE, CHUNK * SLAB)
    lam = jnp.stack([pw_re[CHUNK], pw_im[CHUNK]], axis=1)
    lam = by_slab(lam).transpose(0, 2, 1, 3).reshape(N_SLABS, 2, SLAB_STATE)
    wz_mat, t_mat, v_mat = _ssm_expand(k_rep, wz_rep, v_rep)
    return wz_mat, t_mat, v_mat, lam


def _ssm_expand_kernel(k_ref, wz_ref, v_ref, wz_out, t_out, v_out):
    def own_group(rows, log2_per_row, cols, col_period, log2_per_col):
        a = lax.broadcasted_iota(jnp.int32, (rows, cols), 0) >> log2_per_row
        c = (lax.broadcasted_iota(jnp.int32, (rows, cols), 1) & (col_period - 1)) >> log2_per_col
        return a == c

    def expand(vals, mask):
        return jnp.where(mask, jnp.tile(vals, (SLAB_GROUPS, 1)), 0.0).astype(BF16)

    m_t = own_group(SLAB, 4, SLAB, SLAB, 4)
    lag_blocks = [expand(k_ref[d], m_t) for d in range(CHUNK)]
    for j in range(CHUNK):
        for b in range(CHUNK):
            blk = lag_blocks[b - j] if b >= j else jnp.zeros((SLAB, SLAB), BF16)
            t_out[j * SLAB:(j + 1) * SLAB, b * SLAB:(b + 1) * SLAB] = blk
    m_w = own_group(SLAB, 4, 2 * SLAB_STATE, SLAB_STATE, 6)
    for j in range(CHUNK):
        wz_out[j * SLAB:(j + 1) * SLAB, :] = expand(wz_ref[j], m_w)
    m_v = own_group(SLAB_STATE, 6, CHUNK * SLAB, SLAB, 4)
    for r in range(2):
        v_out[r * SLAB_STATE:(r + 1) * SLAB_STATE, :] = expand(v_ref[r], m_v)


def _ssm_expand(k_rep, wz_rep, v_rep):
    assert SSM_GROUP == 1 << 4 and SSM_STATE == 1 << 6
    mat = jax.ShapeDtypeStruct((N_SLABS, CHUNK * SLAB, CHUNK * SLAB), BF16)
    slab3 = lambda shape: pl.BlockSpec((None,) + shape, lambda k: (k, 0, 0))
    return pl.pallas_call(
        _ssm_expand_kernel,
        grid=(N_SLABS,),
        in_specs=[
            pl.BlockSpec((None, CHUNK, SSM_GROUP, SLAB), lambda k: (k, 0, 0, 0)),
            pl.BlockSpec((None, CHUNK, SSM_GROUP, 2 * SLAB_STATE), lambda k: (k, 0, 0, 0)),
            pl.BlockSpec((None, 2, SSM_STATE, CHUNK * SLAB), lambda k: (k, 0, 0, 0)),
        ],
        out_specs=[slab3((CHUNK * SLAB, 2 * SLAB_STATE)), slab3((CHUNK * SLAB, CHUNK * SLAB)),
                   slab3((2 * SLAB_STATE, CHUNK * SLAB))],
        out_shape=[mat, mat, mat],
        compiler_params=_params("parallel"),
        name="ssm_expand",
    )(k_rep, wz_rep, v_rep)


def _cmul(ar, ai, br, bi):
    return ar * br - ai * bi, ar * bi + ai * br


def _ssm_kernel(*refs):
    u_refs = refs[:CHUNK]
    wz_ref, t_ref, v_ref, lam_ref, o_ref, s_ref = refs[CHUNK:]
    ns = SLAB_STATE
    lam1 = (lam_ref[0:1, :], lam_ref[1:2, :])
    lam2 = _cmul(*lam1, *lam1)
    lam4 = _cmul(*lam2, *lam2)
    lam8 = _cmul(*lam4, *lam4)

    def lhs_rows(r0, n):
        return jnp.concatenate([u[pl.ds(r0, n), :] for u in u_refs], axis=1)

    def pass1(b, _):
        r0 = pl.multiple_of(b * SSM_ROWS, SSM_ROWS)
        z = jnp.dot(lhs_rows(r0, SSM_ROWS), wz_ref[...], preferred_element_type=F32)
        zr, zi = z[:, :ns], z[:, ns:]
        sub = lax.broadcasted_iota(jnp.int32, (SSM_ROWS, ns), 0) & (SUBLANES - 1)
        for shift, (cr, ci) in ((1, lam1), (2, lam2), (4, lam4)):
            keep = sub >= shift
            pr = jnp.where(keep, pltpu.roll(zr, shift, axis=0), 0.0)
            pi = jnp.where(keep, pltpu.roll(zi, shift, axis=0), 0.0)
            ar, ai = _cmul(cr, ci, pr, pi)
            zr, zi = zr + ar, zi + ai
        s_ref[pl.ds(r0, SSM_ROWS), :ns] = zr
        s_ref[pl.ds(r0, SSM_ROWS), ns:] = zi
        return 0

    lax.fori_loop(0, N_CHUNKS // SSM_ROWS, pass1, 0, unroll=2)

    pow_re, pow_im = [jnp.ones((1, ns), F32)], [jnp.zeros((1, ns), F32)]
    for _ in range(SUBLANES - 1):
        nr, ni = _cmul(pow_re[-1], pow_im[-1], *lam1)
        pow_re.append(nr)
        pow_im.append(ni)
    tab_re = jnp.concatenate(pow_re, axis=0)
    tab_im = jnp.concatenate(pow_im, axis=0)
    sub8 = lax.broadcasted_iota(jnp.int32, (SUBLANES, ns), 0)

    def pass2(t, c):
        cr, ci = c
        r0 = pl.multiple_of(t * SUBLANES, SUBLANES)
        lr = s_ref[pl.ds(r0, SUBLANES), :ns]
        li = s_ref[pl.ds(r0, SUBLANES), ns:]
        er = jnp.where(sub8 >= 1, pltpu.roll(lr, 1, axis=0), 0.0)
        ei = jnp.where(sub8 >= 1, pltpu.roll(li, 1, axis=0), 0.0)
        ar, ai = _cmul(tab_re, tab_im, cr, ci)
        s_ref[pl.ds(r0, SUBLANES), :ns] = er + ar
        s_ref[pl.ds(r0, SUBLANES), ns:] = ei + ai
        nr, ni = _cmul(*lam8, cr, ci)
        return lr[SUBLANES - 1:SUBLANES, :] + nr, li[SUBLANES - 1:SUBLANES, :] + ni

    zero = jnp.zeros((1, ns), F32)
    lax.fori_loop(0, N_CHUNKS // SUBLANES, pass2, (zero, zero), unroll=4)

    for b in range(N_CHUNKS // SSM_ROWS):
        r0 = b * SSM_ROWS
        y = jnp.dot(lhs_rows(r0, SSM_ROWS), t_ref[...], preferred_element_type=F32)
        y = y + jnp.dot(s_ref[pl.ds(r0, SSM_ROWS), :].astype(BF16), v_ref[...],
                        preferred_element_type=F32)
        for j in range(CHUNK):
            o_ref[pl.ds(r0 * CHUNK + j, SSM_ROWS, stride=CHUNK), :] = y[:, j * SLAB:(j + 1) * SLAB]


def _ssm(u2, wz_mat, t_mat, v_mat, lam):
    u_specs = [pl.BlockSpec((N_CHUNKS, SLAB), functools.partial(lambda j, k: (0, j * N_SLABS + k), j))
               for j in range(CHUNK)]
    return pl.pallas_call(
        _ssm_kernel,
        grid=(N_SLABS,),
        in_specs=u_specs + [
            pl.BlockSpec((None, CHUNK * SLAB, 2 * SLAB_STATE), lambda k: (k, 0, 0)),
            pl.BlockSpec((None, CHUNK * SLAB, CHUNK * SLAB), lambda k: (k, 0, 0)),
            pl.BlockSpec((None, 2 * SLAB_STATE, CHUNK * SLAB), lambda k: (k, 0, 0)),
            pl.BlockSpec((None, 2, SLAB_STATE), lambda k: (k, 0, 0)),
        ],
        out_specs=pl.BlockSpec((SEQ, SLAB), lambda k: (0, k), pipeline_mode=pl.Buffered(1)),
        out_shape=jax.ShapeDtypeStruct((SEQ, SSM_WIDTH), F32),
        scratch_shapes=[pltpu.VMEM((N_CHUNKS, 2 * SLAB_STATE), F32)],
        compiler_params=_params("arbitrary"),
        name="ssm",
    )(*([u2] * CHUNK), wz_mat, t_mat, v_mat, lam)


def _gelu_tanh(x):
    c = 0.7978845608028654
    return 0.5 * x * (1.0 + jnp.tanh(c * (x + 0.044715 * (x * x * x))))


def _post_kernel(x_ref, oa_ref, y_ref, ga_ref, gb_ref, wglu_ref, pa_ref, ps_ref, wo_ref, o_ref):
    y = _gelu_tanh(y_ref[...].astype(F32))
    yb = y.astype(BF16)
    glu = jnp.dot(yb, wglu_ref[...], preferred_element_type=F32)
    o_ssm = (y * _sigmoid(glu)).astype(BF16)
    a = jnp.dot(oa_ref[...], pa_ref[...], preferred_element_type=F32)
    b = jnp.dot(o_ssm, ps_ref[...], preferred_element_type=F32)
    merged = _sigmoid(ga_ref[...].astype(F32)) * a + _sigmoid(gb_ref[...].astype(F32)) * b
    o_ref[...] = x_ref[...] + jnp.dot(merged.astype(BF16), wo_ref[...], preferred_element_type=F32)


def _post(x, o_attn, y, proj, w_glu, p_attn, p_ssm, w_out):
    tm = 256
    ga_blk = (3 * SB_WIDTH + SSM_WIDTH) // D_MODEL
    const = lambda shape: pl.BlockSpec(shape, lambda i: (0, 0), pipeline_mode=pl.Buffered(1))
    return pl.pallas_call(
        _post_kernel,
        grid=(SEQ // tm,),
        in_specs=[
            pl.BlockSpec((tm, D_MODEL), lambda i: (i, 0)),
            pl.BlockSpec((tm, SB_WIDTH), lambda i: (i, 0)),
            pl.BlockSpec((tm, SSM_WIDTH), lambda i: (i, 0)),
            pl.BlockSpec((tm, D_MODEL), lambda i: (i, ga_blk)),
            pl.BlockSpec((tm, D_MODEL), lambda i: (i, ga_blk + 1)),
            const((SSM_WIDTH, SSM_WIDTH)),
            const((SB_WIDTH, D_MODEL)),
            const((SSM_WIDTH, D_MODEL)),
            const((D_MODEL, D_MODEL)),
        ],
        out_specs=pl.BlockSpec((tm, D_MODEL), lambda i: (i, 0)),
        out_shape=jax.ShapeDtypeStruct((SEQ, D_MODEL), F32),
        compiler_params=_params("parallel"),
        name="post",
    )(x, o_attn, y, proj, proj, w_glu, p_attn, p_ssm, w_out)


def _ffn_kernel(x_ref, g_ref, wg_ref, wu_ref, wd_ref, o_ref, h_ref):
    f = pl.program_id(1)

    @pl.when(f == 0)
    def _():
        x = x_ref[...]
        h_ref[...] = _rms_rows(x, g_ref[...]).astype(BF16)
        o_ref[...] = x

    h = h_ref[...]
    a = jnp.dot(h, wg_ref[...], preferred_element_type=F32)
    b = jnp.dot(h, wu_ref[...], preferred_element_type=F32)
    t = (a * _sigmoid(a) * b).astype(BF16)
    o_ref[...] += jnp.dot(t, wd_ref[...], preferred_element_type=F32)


def _ffn(x, g, w_gate, w_up, w_down):
    tm, tf = 512, 1024
    return pl.pallas_call(
        _ffn_kernel,
        grid=(SEQ // tm, D_FF_DENSE // tf),
        in_specs=[
            pl.BlockSpec((tm, D_MODEL), lambda i, f: (i, 0)),
            pl.BlockSpec((1, D_MODEL), lambda i, f: (0, 0)),
            pl.BlockSpec((D_MODEL, tf), lambda i, f: (0, f)),
            pl.BlockSpec((D_MODEL, tf), lambda i, f: (0, f)),
            pl.BlockSpec((tf, D_MODEL), lambda i, f: (f, 0)),
        ],
        out_specs=pl.BlockSpec((tm, D_MODEL), lambda i, f: (i, 0)),
        out_shape=jax.ShapeDtypeStruct((SEQ, D_MODEL), F32),
        scratch_shapes=[pltpu.VMEM((tm, D_MODEL), BF16)],
        compiler_params=_params("parallel", "arbitrary"),
        name="ffn",
    )(x, g, w_gate, w_up, w_down)


MOE_TM = 512
MOE_SUB = 2
MOE_ROWS = 160
MOE_PAD = 256


def _moe_kernel(x_ref, g_ref, wr_ref, wg_ref, wu_ref, wd_ref, gf_ref, o_ref,
                h_ref, rw_ref, rk_ref, rkt_ref, cnt_ref, y_ref):
    e = pl.program_id(1)
    tm = MOE_TM
    lane = lax.broadcasted_iota(jnp.int32, (tm, LANES), 1)
    tiles = [slice(u * tm, (u + 1) * tm) for u in range(MOE_SUB)]

    @pl.when(e == 0)
    def _():
        before = (lax.broadcasted_iota(jnp.int32, (tm, tm), 1)
                  < lax.broadcasted_iota(jnp.int32, (tm, tm), 0))
        before = jnp.where(before, 1.0, 0.0).astype(BF16)
        for u, rows in enumerate(tiles):
            x = x_ref[rows, :]
            hf = _rms_rows(x, g_ref[...])
            h_hi = hf.astype(BF16)
            h_ref[rows, :] = h_hi
            o_ref[rows, :] = x
            h_lo = (hf - h_hi.astype(F32)).astype(BF16)
            hi = jnp.dot(h_hi, wr_ref[...], preferred_element_type=F32)
            logits = (hi[:, :LANES] + hi[:, LANES:]
                      + jnp.dot(h_lo, wr_ref[:, :LANES], preferred_element_type=F32))
            lg = jnp.where(lane < N_EXPERTS, logits, -jnp.inf)
            m1 = jnp.max(lg, axis=-1, keepdims=True)
            i1 = jnp.min(jnp.where(lg == m1, lane, LANES), axis=-1, keepdims=True)
            lg2 = jnp.where(lane == i1, -jnp.inf, lg)
            m2 = jnp.max(lg2, axis=-1, keepdims=True)
            i2 = jnp.min(jnp.where(lg2 == m2, lane, LANES), axis=-1, keepdims=True)
            e2 = jnp.exp(m2 - m1)
            g1 = 1.0 / (1.0 + e2)
            g2 = e2 / (1.0 + e2)
            rw_ref[rows, :] = jnp.where(lane == i1, g1, 0.0) + jnp.where(lane == i2, g2, 0.0)
            routed = jnp.logical_or(lane == i1, lane == i2)
            onehot = jnp.where(routed, 1.0, 0.0)
            ahead = jnp.dot(before, onehot.astype(BF16), preferred_element_type=F32)
            rk = jnp.where(routed, ahead, -1.0)
            rk_ref[rows, :] = rk
            rkt_ref[u] = rk.T
            cnt_ref[u] = jnp.sum(onehot, axis=0, keepdims=True)
        y_ref[...] = jnp.zeros_like(y_ref)

    lane1 = lax.broadcasted_iota(jnp.int32, (1, LANES), 1)
    row_id = lax.broadcasted_iota(jnp.int32, (MOE_ROWS, tm), 0).astype(F32)
    col_id = lax.broadcasted_iota(jnp.int32, (tm, MOE_PAD), 1).astype(F32)

    def route_tile(u, rows):
        pick = lambda ref: jnp.sum(jnp.where(lane == e, ref[rows, :], 0.0), axis=-1, keepdims=True)
        rk_col, gate_col = pick(rk_ref), pick(rw_ref)
        rk_row = rkt_ref[u, pl.ds(e, 1), :]
        count = jnp.sum(jnp.where(lane1 == e, cnt_ref[u], 0.0)).astype(jnp.int32)

        def one_pass(s, _):
            base = (s * MOE_ROWS).astype(F32)
            sel = jnp.where(rk_row - base == row_id, 1.0, 0.0).astype(BF16)
            xc = jnp.dot(sel, h_ref[rows, :], preferred_element_type=F32).astype(BF16)
            a = jnp.dot(xc, wg_ref[...], preferred_element_type=F32)
            b = jnp.dot(xc, wu_ref[...], preferred_element_type=F32)
            t = (a * _sigmoid(a) * b).astype(BF16)
            y_ref[:MOE_ROWS, :] = jnp.dot(t, wd_ref[...], preferred_element_type=F32).astype(BF16)
            back = jnp.where(rk_col - base == col_id, 1.0, 0.0).astype(BF16)
            o_ref[rows, :] += gate_col * jnp.dot(back, y_ref[...], preferred_element_type=F32)
            return 0

        lax.fori_loop(0, (count + MOE_ROWS - 1) // MOE_ROWS, one_pass, 0)

    for u, rows in enumerate(tiles):
        route_tile(u, rows)

    @pl.when(e == pl.num_programs(1) - 1)
    def _():
        for rows in tiles:
            o_ref[rows, :] = _rms_rows(o_ref[rows, :], gf_ref[...])


def _moe(x, g, w_router, w_gate, w_up, w_down, g_final):
    tm = MOE_TM * MOE_SUB
    once = pl.Buffered(1)
    return pl.pallas_call(
        _moe_kernel,
        grid=(SEQ // tm, N_EXPERTS),
        in_specs=[
            pl.BlockSpec((tm, D_MODEL), lambda i, e: (i, 0), pipeline_mode=once),
            pl.BlockSpec((1, D_MODEL), lambda i, e: (0, 0)),
            pl.BlockSpec((D_MODEL, 2 * LANES), lambda i, e: (0, 0)),
            pl.BlockSpec((None, D_MODEL, D_FF_EXPERT), lambda i, e: (e, 0, 0)),
            pl.BlockSpec((None, D_MODEL, D_FF_EXPERT), lambda i, e: (e, 0, 0)),
            pl.BlockSpec((None, D_FF_EXPERT, D_MODEL), lambda i, e: (e, 0, 0)),
            pl.BlockSpec((1, D_MODEL), lambda i, e: (0, 0)),
        ],
        out_specs=pl.BlockSpec((tm, D_MODEL), lambda i, e: (i, 0), pipeline_mode=once),
        out_shape=jax.ShapeDtypeStruct((SEQ, D_MODEL), F32),
        scratch_shapes=[
            pltpu.VMEM((tm, D_MODEL), BF16),
            pltpu.VMEM((tm, LANES), F32),
            pltpu.VMEM((tm, LANES), F32),
            pltpu.VMEM((MOE_SUB, LANES, MOE_TM), F32),
            pltpu.VMEM((MOE_SUB, 1, LANES), F32),
            pltpu.VMEM((MOE_PAD, D_MODEL), BF16),
        ],
        compiler_params=_params("parallel", "arbitrary"),
        name="moe",
    )(x, g, w_router, w_gate, w_up, w_down, g_final)


def kernel(x, mix_norm, ffn_norm, final_norm, w_in, ssm_a_re, ssm_a_im, ssm_log_dt, ssm_b_re, ssm_b_im,
           ssm_c_re, ssm_c_im, ssm_d, w_glu, p_attn, p_ssm, w_out, ffn_w_gate, ffn_w_up, ffn_w_down,
           w_router, moe_w_gate, moe_w_up, moe_w_down):
    assert x.shape == (1, SEQ, D_MODEL) and DEPTH == 2
    xs = x.reshape(SEQ, D_MODEL)
    col_scale = jnp.concatenate([jnp.full((SB_WIDTH,), HEAD_DIM ** -0.5 * LOG2E, F32),
                                 jnp.ones((IN_COLS - SB_WIDTH,), F32)])
    for layer in range(DEPTH):
        w_in_b = (w_in[layer] * col_scale).astype(BF16)
        proj, u, vt = _inproj(xs, mix_norm[layer][None], w_in_b)
        o_attn = _attention(proj, vt)
        ssm_w = _ssm_weights(ssm_a_re[layer], ssm_a_im[layer], ssm_log_dt[layer], ssm_b_re[layer],
                             ssm_b_im[layer], ssm_c_re[layer], ssm_c_im[layer], ssm_d[layer])
        y = _ssm(u, *ssm_w)
        xs = _post(xs, o_attn, y, proj, w_glu[layer].astype(BF16), p_attn[layer].astype(BF16),
                   p_ssm[layer].astype(BF16), w_out[layer].astype(BF16))
        i = layer // 2
        if layer % 2 == 0:
            xs = _ffn(xs, ffn_norm[layer][None], ffn_w_gate[i].astype(BF16), ffn_w_up[i].astype(BF16),
                      ffn_w_down[i].astype(BF16))
        else:
            wr = jnp.pad(w_router[i], ((0, 0), (0, LANES - N_EXPERTS)))
            wr_hi = wr.astype(BF16)
            wr = jnp.concatenate([wr_hi, (wr - wr_hi.astype(F32)).astype(BF16)], axis=1)
            xs = _moe(xs, ffn_norm[layer][None], wr, moe_w_gate[i].astype(BF16), moe_w_up[i].astype(BF16),
                      moe_w_down[i].astype(BF16), final_norm[None])
    return xs.reshape(1, SEQ, D_MODEL)
```

```python
import functools

import jax
import jax.numpy as jnp
from jax import lax
from jax.experimental import pallas as pl
from jax.experimental.pallas import tpu as pltpu

F32 = jnp.float32
BF16 = jnp.bfloat16

D_MODEL = 2048
SEQ = 16384
DEPTH = 2
HEAD_DIM = 128
HEADS = 8
SB_WIDTH = HEADS * HEAD_DIM
SSM_WIDTH = 1024
SSM_GROUP = 16
SSM_GROUPS = 64
SSM_STATE = 64
IN_COLS = 3 * SB_WIDTH + SSM_WIDTH + 2 * D_MODEL
D_FF_DENSE = 2 * D_MODEL
N_EXPERTS = 8
D_FF_EXPERT = D_MODEL // 2
RMS_EPS = 1e-6

LANES = 128
SUBLANES = 8
VMEM_LIMIT = 56 * 1024 * 1024

TQ = 256
TK = 256
HP = 4
QB = 2
LOG2E = 1.4426950408889634
UNDERFLOW_LOG2 = 152.0
NORM_MARGIN = 1.01
CHUNK = 8
N_CHUNKS = SEQ // CHUNK
SLAB = LANES
N_SLABS = SSM_WIDTH // SLAB
SLAB_GROUPS = SLAB // SSM_GROUP
SLAB_STATE = SLAB_GROUPS * SSM_STATE
SSM_ROWS = 256


def _params(*sem):
    return pltpu.CompilerParams(dimension_semantics=sem, vmem_limit_bytes=VMEM_LIMIT)


def _rms_rows(x, g):
    ms = jnp.mean(x * x, axis=-1, keepdims=True)
    return x * lax.rsqrt(ms + RMS_EPS) * g


def _sigmoid(x):
    return 1.0 / (1.0 + jnp.exp(-x))


def _inproj_kernel(x_ref, g_ref, w_ref, p_ref, u_ref, vt_ref, h_ref, s_ref):
    j = pl.program_id(1)
    tm = x_ref.shape[0]
    project = lambda h: jnp.dot(h, w_ref[...], preferred_element_type=F32)

    @pl.when(j == 0)
    def _():
        piece = tm // 4
        for rows in (slice(c * piece, (c + 1) * piece) for c in range(4)):
            h = _rms_rows(x_ref[rows, :], g_ref[...]).astype(BF16)
            h_ref[rows, :] = h
            p_ref[rows, :] = project(h).astype(BF16)

    @pl.when(j == 1)
    def _():
        slabs = s_ref.shape[0]
        for b in range(tm // TK):
            rows = slice(b * TK, (b + 1) * TK)
            r = project(h_ref[rows, :])
            p_ref[rows, :] = r.astype(BF16)
            rt = r[:, :SB_WIDTH].T
            for h in range(HEADS):
                vt_ref[h, b] = rt[h * HEAD_DIM:(h + 1) * HEAD_DIM, :].astype(BF16)
            for l in range(slabs):
                s_ref[l, rows, :] = r[:, SB_WIDTH + l * LANES:SB_WIDTH + (l + 1) * LANES]
            out_rows = TK // CHUNK
            u_ref[b * out_rows:(b + 1) * out_rows, :] = jnp.concatenate(
                [s_ref[l, pl.ds(b * TK + c, out_rows, stride=CHUNK), :]
                 for c in range(CHUNK) for l in range(slabs)], axis=1).astype(BF16)

    @pl.when(j >= 2)
    def _():
        p_ref[...] = project(h_ref[...]).astype(BF16)


def _inproj(x, g, w):
    tm, tn = 512, 2048
    assert tn == 2 * SB_WIDTH == SB_WIDTH + SSM_WIDTH and tm % TK == 0 and tm % CHUNK == 0
    return pl.pallas_call(
        _inproj_kernel,
        grid=(SEQ // tm, IN_COLS // tn),
        in_specs=[
            pl.BlockSpec((tm, D_MODEL), lambda i, j: (i, 0)),
            pl.BlockSpec((1, D_MODEL), lambda i, j: (0, 0)),
            pl.BlockSpec((D_MODEL, tn), lambda i, j: (0, j)),
        ],
        out_specs=[
            pl.BlockSpec((tm, tn), lambda i, j: (i, j)),
            pl.BlockSpec((tm // CHUNK, CHUNK * SSM_WIDTH), lambda i, j: (i, 0)),
            pl.BlockSpec((HEADS, tm // TK, HEAD_DIM, TK), lambda i, j: (0, i, 0, 0)),
        ],
        out_shape=[
            jax.ShapeDtypeStruct((SEQ, IN_COLS), BF16),
            jax.ShapeDtypeStruct((N_CHUNKS, CHUNK * SSM_WIDTH), BF16),
            jax.ShapeDtypeStruct((HEADS, SEQ // TK, HEAD_DIM, TK), BF16),
        ],
        scratch_shapes=[pltpu.VMEM((tm, D_MODEL), BF16), pltpu.VMEM((SSM_WIDTH // LANES, tm, LANES), F32)],
        compiler_params=_params("parallel", "arbitrary"),
        name="inproj",
    )(x, g, w)


def _attn_kernel(q_ref, k_ref, vt_ref, o_ref, kmax_ref, carry_ref, acc_ref):
    i = pl.program_id(1)
    hs = lambda h: slice(h * HEAD_DIM, (h + 1) * HEAD_DIM)
    ones = jnp.ones((HEAD_DIM, LANES), BF16)

    def row_norm2(rows_bf16):
        sq = rows_bf16.astype(F32)
        return jnp.dot((sq * sq).astype(BF16), ones, preferred_element_type=F32)

    @pl.when(i == 0)
    def _():
        def body(j, run):
            ks = pl.multiple_of(j * TK, TK)
            new, rows = [], []
            for h in range(HP):
                n2 = jnp.max(row_norm2(k_ref[pl.ds(ks, TK), hs(h)]), axis=0, keepdims=True)[:, :1]
                new.append(jnp.maximum(run[h], n2))
                rows.append(jnp.broadcast_to(jnp.sqrt(new[h]), (1, TQ)))
            kmax_ref[j] = jnp.concatenate(rows, axis=0)
            return tuple(new)

        lax.fori_loop(0, SEQ // TK, body, tuple(jnp.zeros((1, 1), F32) for _ in range(HP)), unroll=4)

    row = lax.broadcasted_iota(jnp.int32, (TK, TQ), 0)
    col = lax.broadcasted_iota(jnp.int32, (TK, TQ), 1)
    upper = jnp.where(col >= row, 1.0, 0.0).astype(BF16)
    causal = row < col

    def block(h, qrows, j, carry, acc, masked):
        ks = pl.multiple_of(j * TK, TK)
        kb = k_ref[pl.ds(ks, TK), hs(h)]
        zt = lax.dot_general(kb, q_ref[qrows, hs(h)], (((1,), (1,)), ((), ())),
                             preferred_element_type=F32)
        sp = jnp.maximum(zt, 0.0) + jnp.log(1.0 + jnp.exp2(-jnp.abs(zt))) * LOG2E
        if masked:
            sp = jnp.where(causal, sp, 0.0)
        rt = jnp.dot(upper, sp.astype(BF16), preferred_element_type=F32)
        w = jnp.exp2(zt - rt - carry)
        if masked:
            w = jnp.where(causal, w, 0.0)
        acc = acc + jnp.dot(vt_ref[h, j], w.astype(BF16), preferred_element_type=F32)
        return carry + rt[0:1, :], acc

    q_blocks = [(qq, i * QB + qq, slice(qq * TQ, (qq + 1) * TQ)) for qq in range(QB)]
    qmax = {}
    for qq, qi, qrows in q_blocks:
        prev = jnp.maximum(qi - 1, 0)
        no_prev = jnp.where(qi == 0, 1e30, 0.0).astype(F32)
        for h in range(HP):
            carry, acc = block(h, qrows, qi, jnp.zeros((1, TQ), F32), jnp.zeros((HEAD_DIM, TQ), F32), True)
            carry, acc = block(h, qrows, prev, carry + no_prev, acc, False)
            carry_ref[qq, h] = carry
            acc_ref[qq, h] = acc
            qf = q_ref[qrows, hs(h)].astype(F32)
            n2 = jnp.max(jnp.sum(qf * qf, axis=1, keepdims=True), axis=0, keepdims=True)
            qmax[qq, h] = jnp.sqrt(n2) * NORM_MARGIN

    def walk_back(qq, qi, qrows):
        def more_needed(j):
            km = kmax_ref[j]
            worst = None
            for h in range(HP):
                slack = carry_ref[qq, h] - (UNDERFLOW_LOG2 + qmax[qq, h] * km[h:h + 1, :])
                worst = slack if worst is None else jnp.minimum(worst, slack)
            return (jnp.min(worst) < 0.0).astype(jnp.int32)

        def cond(c):
            return jnp.logical_and(c[0] >= 0, c[1] > 0)

        def body(c):
            j = c[0]
            for h in range(HP):
                carry, acc = block(h, qrows, j, carry_ref[qq, h], acc_ref[qq, h], False)
                carry_ref[qq, h] = carry
                acc_ref[qq, h] = acc
            return j - 1, more_needed(jnp.maximum(j - 1, 0))

        lax.while_loop(cond, body, (qi - 2, more_needed(jnp.maximum(qi - 2, 0))))

    for qq, qi, qrows in q_blocks:
        walk_back(qq, qi, qrows)
        for h in range(HP):
            o_ref[qrows, hs(h)] = acc_ref[qq, h].T.astype(o_ref.dtype)


def _attention(proj, vt):
    wide = HP * HEAD_DIM
    once = pl.Buffered(1)
    return pl.pallas_call(
        _attn_kernel,
        grid=(HEADS // HP, SEQ // (QB * TQ)),
        in_specs=[
            pl.BlockSpec((QB * TQ, wide), lambda g, i: (i, g)),
            pl.BlockSpec((SEQ, wide), lambda g, i: (0, HEADS // HP + g), pipeline_mode=once),
            pl.BlockSpec((HP, SEQ // TK, HEAD_DIM, TK), lambda g, i: (g, 0, 0, 0), pipeline_mode=once),
        ],
        out_specs=pl.BlockSpec((QB * TQ, wide), lambda g, i: (i, g)),
        out_shape=jax.ShapeDtypeStruct((SEQ, SB_WIDTH), BF16),
        scratch_shapes=[
            pltpu.VMEM((SEQ // TK, HP, TQ), F32),
            pltpu.VMEM((QB, HP, 1, TQ), F32),
            pltpu.VMEM((QB, HP, HEAD_DIM, TQ), F32),
        ],
        compiler_params=_params("arbitrary", "arbitrary"),
        name="attention",
    )(proj, proj, vt)


def _ssm_weights(a_re, a_im, log_dt, b_re, b_im, c_re, c_im, d_skip):
    dt = jnp.exp(log_dt)[:, None]
    d = jnp.arange(CHUNK + 1, dtype=F32)[:, None, None]
    mag = jnp.exp(d * (dt * a_re))
    pw_re = mag * jnp.cos(d * (dt * a_im))
    pw_im = mag * jnp.sin(d * (dt * a_im))
    abar_re, abar_im = pw_re[1], pw_im[1]
    den = a_re * a_re + a_im * a_im
    num_re = abar_re - 1.0
    zoh_re = (num_re * a_re + abar_im * a_im) / den
    zoh_im = (abar_im * a_re - num_re * a_im) / den
    bb_re = zoh_re[..., None] * b_re - zoh_im[..., None] * b_im
    bb_im = zoh_re[..., None] * b_im + zoh_im[..., None] * b_re
    ca_re = c_re[None] * pw_re[:, :, None, :] - c_im[None] * pw_im[:, :, None, :]
    ca_im = c_re[None] * pw_im[:, :, None, :] + c_im[None] * pw_re[:, :, None, :]
    kd = jnp.einsum('dgon,gnp->dgop', ca_re, bb_re) - jnp.einsum('dgon,gnp->dgop', ca_im, bb_im)
    kd = kd.at[0].add(jnp.eye(SSM_GROUP, dtype=F32)[None] * d_skip[:, :, None])
    by_slab = lambda t: t.reshape((N_SLABS, SLAB_GROUPS) + t.shape[1:])
    k_rep = by_slab(kd[:CHUNK].transpose(1, 0, 3, 2)).transpose(0, 2, 3, 1, 4)
    k_rep = k_rep.reshape(N_SLABS, CHUNK, SSM_GROUP, SLAB)
    rev_re, rev_im = pw_re[CHUNK - 1::-1], pw_im[CHUNK - 1::-1]
    wz_re = rev_re[:, :, :, None] * bb_re[None] - rev_im[:, :, :, None] * bb_im[None]
    wz_im = rev_re[:, :, :, None] * bb_im[None] + rev_im[:, :, :, None] * bb_re[None]
    wz = jnp.stack([wz_re, wz_im], axis=0).transpose(2, 1, 4, 0, 3)
    wz_rep = by_slab(wz).transpose(0, 2, 3, 4, 1, 5).reshape(N_SLABS, CHUNK, SSM_GROUP, 2 * SLAB_STATE)
    v = jnp.stack([ca_re[1:], -ca_im[1:]], axis=0).transpose(2, 0, 4, 1, 3)
    v_rep = by_slab(v).transpose(0, 2, 3, 4, 1, 5).reshape(N_SLABS, 2, SSM_STATE, CHUNK * SLAB)
    lam = jnp.stack([pw_re[CHUNK], pw_im[CHUNK]], axis=1)
    lam = by_slab(lam).transpose(0, 2, 1, 3).reshape(N_SLABS, 2, SLAB_STATE)
    wz_mat, t_mat, v_mat = _ssm_expand(k_rep, wz_rep, v_rep)
    return wz_mat, t_mat, v_mat, lam


def _ssm_expand_kernel(k_ref, wz_ref, v_ref, wz_out, t_out, v_out):
    def own_group(rows, log2_per_row, cols, col_period, log2_per_col):
        a = lax.broadcasted_iota(jnp.int32, (rows, cols), 0) >> log2_per_row
        c = (lax.broadcasted_iota(jnp.int32, (rows, cols), 1) & (col_period - 1)) >> log2_per_col
        return a == c

    def expand(vals, mask):
        return jnp.where(mask, jnp.tile(vals, (SLAB_GROUPS, 1)), 0.0).astype(BF16)

    m_t = own_group(SLAB, 4, SLAB, SLAB, 4)
    lag_blocks = [expand(k_ref[d], m_t) for d in range(CHUNK)]
    for j in range(CHUNK):
        for b in range(CHUNK):
            blk = lag_blocks[b - j] if b >= j else jnp.zeros((SLAB, SLAB), BF16)
            t_out[j * SLAB:(j + 1) * SLAB, b * SLAB:(b + 1) * SLAB] = blk
    m_w = own_group(SLAB, 4, 2 * SLAB_STATE, SLAB_STATE, 6)
    for j in range(CHUNK):
        wz_out[j * SLAB:(j + 1) * SLAB, :] = expand(wz_ref[j], m_w)
    m_v = own_group(SLAB_STATE, 6, CHUNK * SLAB, SLAB, 4)
    for r in range(2):
        v_out[r * SLAB_STATE:(r + 1) * SLAB_STATE, :] = expand(v_ref[r], m_v)


def _ssm_expand(k_rep, wz_rep, v_rep):
    assert SSM_GROUP == 1 << 4 and SSM_STATE == 1 << 6
    mat = jax.ShapeDtypeStruct((N_SLABS, CHUNK * SLAB, CHUNK * SLAB), BF16)
    slab3 = lambda shape: pl.BlockSpec((None,) + shape, lambda k: (k, 0, 0))
    return pl.pallas_call(
        _ssm_expand_kernel,
        grid=(N_SLABS,),
        in_specs=[
            pl.BlockSpec((None, CHUNK, SSM_GROUP, SLAB), lambda k: (k, 0, 0, 0)),
            pl.BlockSpec((None, CHUNK, SSM_GROUP, 2 * SLAB_STATE), lambda k: (k, 0, 0, 0)),
            pl.BlockSpec((None, 2, SSM_STATE, CHUNK * SLAB), lambda k: (k, 0, 0, 0)),
        ],
        out_specs=[slab3((CHUNK * SLAB, 2 * SLAB_STATE)), slab3((CHUNK * SLAB, CHUNK * SLAB)),
                   slab3((2 * SLAB_STATE, CHUNK * SLAB))],
        out_shape=[mat, mat, mat],
        compiler_params=_params("parallel"),
        name="ssm_expand",
    )(k_rep, wz_rep, v_rep)


def _cmul(ar, ai, br, bi):
    return ar * br - ai * bi, ar * bi + ai * br


def _ssm_kernel(*refs):
    u_refs = refs[:CHUNK]
    wz_ref, t_ref, v_ref, lam_ref, o_ref, s_ref = refs[CHUNK:]
    ns = SLAB_STATE
    lam1 = (lam_ref[0:1, :], lam_ref[1:2, :])
    lam2 = _cmul(*lam1, *lam1)
    lam4 = _cmul(*lam2, *lam2)
    lam8 = _cmul(*lam4, *lam4)

    def lhs_rows(r0, n):
        return jnp.concatenate([u[pl.ds(r0, n), :] for u in u_refs], axis=1)

    def pass1(b, _):
        r0 = pl.multiple_of(b * SSM_ROWS, SSM_ROWS)
        z = jnp.dot(lhs_rows(r0, SSM_ROWS), wz_ref[...], preferred_element_type=F32)
        zr, zi = z[:, :ns], z[:, ns:]
        sub = lax.broadcasted_iota(jnp.int32, (SSM_ROWS, ns), 0) & (SUBLANES - 1)
        for shift, (cr, ci) in ((1, lam1), (2, lam2), (4, lam4)):
            keep = sub >= shift
            pr = jnp.where(keep, pltpu.roll(zr, shift, axis=0), 0.0)
            pi = jnp.where(keep, pltpu.roll(zi, shift, axis=0), 0.0)
            ar, ai = _cmul(cr, ci, pr, pi)
            zr, zi = zr + ar, zi + ai
        s_ref[pl.ds(r0, SSM_ROWS), :ns] = zr
        s_ref[pl.ds(r0, SSM_ROWS), ns:] = zi
        return 0

    lax.fori_loop(0, N_CHUNKS // SSM_ROWS, pass1, 0, unroll=2)

    pow_re, pow_im = [jnp.ones((1, ns), F32)], [jnp.zeros((1, ns), F32)]
    for _ in range(SUBLANES - 1):
        nr, ni = _cmul(pow_re[-1], pow_im[-1], *lam1)
        pow_re.append(nr)
        pow_im.append(ni)
    tab_re = jnp.concatenate(pow_re, axis=0)
    tab_im = jnp.concatenate(pow_im, axis=0)
    sub8 = lax.broadcasted_iota(jnp.int32, (SUBLANES, ns), 0)

    def pass2(t, c):
        cr, ci = c
        r0 = pl.multiple_of(t * SUBLANES, SUBLANES)
        lr = s_ref[pl.ds(r0, SUBLANES), :ns]
        li = s_ref[pl.ds(r0, SUBLANES), ns:]
        er = jnp.where(sub8 >= 1, pltpu.roll(lr, 1, axis=0), 0.0)
        ei = jnp.where(sub8 >= 1, pltpu.roll(li, 1, axis=0), 0.0)
        ar, ai = _cmul(tab_re, tab_im, cr, ci)
        s_ref[pl.ds(r0, SUBLANES), :ns] = er + ar
        s_ref[pl.ds(r0, SUBLANES), ns:] = ei + ai
        nr, ni = _cmul(*lam8, cr, ci)
        return lr[SUBLANES - 1:SUBLANES, :] + nr, li[SUBLANES - 1:SUBLANES, :] + ni

    zero = jnp.zeros((1, ns), F32)
    lax.fori_loop(0, N_CHUNKS // SUBLANES, pass2, (zero, zero), unroll=4)

    for b in range(N_CHUNKS // SSM_ROWS):
        r0 = b * SSM_ROWS
        y = jnp.dot(lhs_rows(r0, SSM_ROWS), t_ref[...], preferred_element_type=F32)
        y = y + jnp.dot(s_ref[pl.ds(r0, SSM_ROWS), :].astype(BF16), v_ref[...],
                        preferred_element_type=F32)
        for j in range(CHUNK):
            o_ref[pl.ds(r0 * CHUNK + j, SSM_ROWS, stride=CHUNK), :] = y[:, j * SLAB:(j + 1) * SLAB]


def _ssm(u2, wz_mat, t_mat, v_mat, lam):
    u_specs = [pl.BlockSpec((N_CHUNKS, SLAB), functools.partial(lambda j, k: (0, j * N_SLABS + k), j))
               for j in range(CHUNK)]
    return pl.pallas_call(
        _ssm_kernel,
        grid=(N_SLABS,),
        in_specs=u_specs + [
            pl.BlockSpec((None, CHUNK * SLAB, 2 * SLAB_STATE), lambda k: (k, 0, 0)),
            pl.BlockSpec((None, CHUNK * SLAB, CHUNK * SLAB), lambda k: (k, 0, 0)),
            pl.BlockSpec((None, 2 * SLAB_STATE, CHUNK * SLAB), lambda k: (k, 0, 0)),
            pl.BlockSpec((None, 2, SLAB_STATE), lambda k: (k, 0, 0)),
        ],
        out_specs=pl.BlockSpec((SEQ, SLAB), lambda k: (0, k), pipeline_mode=pl.Buffered(1)),
        out_shape=jax.ShapeDtypeStruct((SEQ, SSM_WIDTH), F32),
        scratch_shapes=[pltpu.VMEM((N_CHUNKS, 2 * SLAB_STATE), F32)],
        compiler_params=_params("arbitrary"),
        name="ssm",
    )(*([u2] * CHUNK), wz_mat, t_mat, v_mat, lam)


def _gelu_tanh(x):
    c = 0.7978845608028654
    return 0.5 * x * (1.0 + jnp.tanh(c * (x + 0.044715 * (x * x * x))))


def _post_kernel(x_ref, oa_ref, y_ref, ga_ref, gb_ref, wglu_ref, pa_ref, ps_ref, wo_ref, o_ref):
    y = _gelu_tanh(y_ref[...].astype(F32))
    yb = y.astype(BF16)
    glu = jnp.dot(yb, wglu_ref[...], preferred_element_type=F32)
    o_ssm = (y * _sigmoid(glu)).astype(BF16)
    a = jnp.dot(oa_ref[...], pa_ref[...], preferred_element_type=F32)
    b = jnp.dot(o_ssm, ps_ref[...], preferred_element_type=F32)
    merged = _sigmoid(ga_ref[...].astype(F32)) * a + _sigmoid(gb_ref[...].astype(F32)) * b
    o_ref[...] = x_ref[...] + jnp.dot(merged.astype(BF16), wo_ref[...], preferred_element_type=F32)


def _post(x, o_attn, y, proj, w_glu, p_attn, p_ssm, w_out):
    tm = 256
    ga_blk = (3 * SB_WIDTH + SSM_WIDTH) // D_MODEL
    const = lambda shape: pl.BlockSpec(shape, lambda i: (0, 0), pipeline_mode=pl.Buffered(1))
    return pl.pallas_call(
        _post_kernel,
        grid=(SEQ // tm,),
        in_specs=[
            pl.BlockSpec((tm, D_MODEL), lambda i: (i, 0)),
            pl.BlockSpec((tm, SB_WIDTH), lambda i: (i, 0)),
            pl.BlockSpec((tm, SSM_WIDTH), lambda i: (i, 0)),
            pl.BlockSpec((tm, D_MODEL), lambda i: (i, ga_blk)),
            pl.BlockSpec((tm, D_MODEL), lambda i: (i, ga_blk + 1)),
            const((SSM_WIDTH, SSM_WIDTH)),
            const((SB_WIDTH, D_MODEL)),
            const((SSM_WIDTH, D_MODEL)),
            const((D_MODEL, D_MODEL)),
        ],
        out_specs=pl.BlockSpec((tm, D_MODEL), lambda i: (i, 0)),
        out_shape=jax.ShapeDtypeStruct((SEQ, D_MODEL), F32),
        compiler_params=_params("parallel"),
        name="post",
    )(x, o_attn, y, proj, proj, w_glu, p_attn, p_ssm, w_out)


def _ffn_kernel(x_ref, g_ref, wg_ref, wu_ref, wd_ref, o_ref, h_ref):
    f = pl.program_id(1)

    def swiglu(h):
        a = jnp.dot(h, wg_ref[...], preferred_element_type=F32)
        b = jnp.dot(h, wu_ref[...], preferred_element_type=F32)
        t = (a * _sigmoid(a) * b).astype(BF16)
        return jnp.dot(t, wd_ref[...], preferred_element_type=F32)

    @pl.when(f == 0)
    def _():
        piece = x_ref.shape[0] // 4
        for rows in (slice(c * piece, (c + 1) * piece) for c in range(4)):
            x = x_ref[rows, :]
            h = _rms_rows(x, g_ref[...]).astype(BF16)
            h_ref[rows, :] = h
            o_ref[rows, :] = x + swiglu(h)

    @pl.when(f > 0)
    def _():
        o_ref[...] += swiglu(h_ref[...])


def _ffn(x, g, w_gate, w_up, w_down):
    tm, tf = 512, 1024
    return pl.pallas_call(
        _ffn_kernel,
        grid=(SEQ // tm, D_FF_DENSE // tf),
        in_specs=[
            pl.BlockSpec((tm, D_MODEL), lambda i, f: (i, 0)),
            pl.BlockSpec((1, D_MODEL), lambda i, f: (0, 0)),
            pl.BlockSpec((D_MODEL, tf), lambda i, f: (0, f)),
            pl.BlockSpec((D_MODEL, tf), lambda i, f: (0, f)),
            pl.BlockSpec((tf, D_MODEL), lambda i, f: (f, 0)),
        ],
        out_specs=pl.BlockSpec((tm, D_MODEL), lambda i, f: (i, 0)),
        out_shape=jax.ShapeDtypeStruct((SEQ, D_MODEL), F32),
        scratch_shapes=[pltpu.VMEM((tm, D_MODEL), BF16)],
        compiler_params=_params("parallel", "arbitrary"),
        name="ffn",
    )(x, g, w_gate, w_up, w_down)


MOE_TM = 512
MOE_SUB = 2
MOE_ROWS = 160
MOE_PAD = 256


def _moe_kernel(x_ref, g_ref, wr_ref, wg_ref, wu_ref, wd_ref, gf_ref, o_ref,
                h_ref, rw_ref, rk_ref, rkt_ref, cnt_ref, y_ref):
    e = pl.program_id(1)
    tm = MOE_TM
    lane = lax.broadcasted_iota(jnp.int32, (tm, LANES), 1)
    tiles = [slice(u * tm, (u + 1) * tm) for u in range(MOE_SUB)]

    @pl.when(e == 0)
    def _():
        before = (lax.broadcasted_iota(jnp.int32, (tm, tm), 1)
                  < lax.broadcasted_iota(jnp.int32, (tm, tm), 0))
        before = jnp.where(before, 1.0, 0.0).astype(BF16)
        for u, rows in enumerate(tiles):
            x = x_ref[rows, :]
            hf = _rms_rows(x, g_ref[...])
            h_hi = hf.astype(BF16)
            h_ref[rows, :] = h_hi
            o_ref[rows, :] = x
            h_lo = (hf - h_hi.astype(F32)).astype(BF16)
            hi = jnp.dot(h_hi, wr_ref[...], preferred_element_type=F32)
            logits = (hi[:, :LANES] + hi[:, LANES:]
                      + jnp.dot(h_lo, wr_ref[:, :LANES], preferred_element_type=F32))
            lg = jnp.where(lane < N_EXPERTS, logits, -jnp.inf)
            m1 = jnp.max(lg, axis=-1, keepdims=True)
            i1 = jnp.min(jnp.where(lg == m1, lane, LANES), axis=-1, keepdims=True)
            lg2 = jnp.where(lane == i1, -jnp.inf, lg)
            m2 = jnp.max(lg2, axis=-1, keepdims=True)
            i2 = jnp.min(jnp.where(lg2 == m2, lane, LANES), axis=-1, keepdims=True)
            e2 = jnp.exp(m2 - m1)
            g1 = 1.0 / (1.0 + e2)
            g2 = e2 / (1.0 + e2)
            rw_ref[rows, :] = jnp.where(lane == i1, g1, 0.0) + jnp.where(lane == i2, g2, 0.0)
            routed = jnp.logical_or(lane == i1, lane == i2)
            onehot = jnp.where(routed, 1.0, 0.0)
            ahead = jnp.dot(before, onehot.astype(BF16), preferred_element_type=F32)
            rk = jnp.where(routed, ahead, -1.0)
            rk_ref[rows, :] = rk
            rkt_ref[u] = rk.T
            cnt_ref[u] = jnp.sum(onehot, axis=0, keepdims=True)
        y_ref[...] = jnp.zeros_like(y_ref)

    lane1 = lax.broadcasted_iota(jnp.int32, (1, LANES), 1)
    row_id = lax.broadcasted_iota(jnp.int32, (MOE_ROWS, tm), 0).astype(F32)
    col_id = lax.broadcasted_iota(jnp.int32, (tm, MOE_PAD), 1).astype(F32)

    def route_tile(u, rows):
        pick = lambda ref: jnp.sum(jnp.where(lane == e, ref[rows, :], 0.0), axis=-1, keepdims=True)
        rk_col, gate_col = pick(rk_ref), pick(rw_ref)
        rk_row = rkt_ref[u, pl.ds(e, 1), :]
        count = jnp.sum(jnp.where(lane1 == e, cnt_ref[u], 0.0)).astype(jnp.int32)

        def one_pass(s, _):
            base = (s * MOE_ROWS).astype(F32)
            sel = jnp.where(rk_row - base == row_id, 1.0, 0.0).astype(BF16)
            xc = jnp.dot(sel, h_ref[rows, :], preferred_element_type=F32).astype(BF16)
            a = jnp.dot(xc, wg_ref[...], preferred_element_type=F32)
            b = jnp.dot(xc, wu_ref[...], preferred_element_type=F32)
            t = (a * _sigmoid(a) * b).astype(BF16)
            y_ref[:MOE_ROWS, :] = jnp.dot(t, wd_ref[...], preferred_element_type=F32).astype(BF16)
            back = jnp.where(rk_col - base == col_id, 1.0, 0.0).astype(BF16)
            o_ref[rows, :] += gate_col * jnp.dot(back, y_ref[...], preferred_element_type=F32)
            return 0

        lax.fori_loop(0, (count + MOE_ROWS - 1) // MOE_ROWS, one_pass, 0)

    for u, rows in enumerate(tiles):
        route_tile(u, rows)

    @pl.when(e == pl.num_programs(1) - 1)
    def _():
        for rows in tiles:
            o_ref[rows, :] = _rms_rows(o_ref[rows, :], gf_ref[...])


def _moe(x, g, w_router, w_gate, w_up, w_down, g_final):
    tm = MOE_TM * MOE_SUB
    once = pl.Buffered(1)
    return pl.pallas_call(
        _moe_kernel,
        grid=(SEQ // tm, N_EXPERTS),
        in_specs=[
            pl.BlockSpec((tm, D_MODEL), lambda i, e: (i, 0), pipeline_mode=once),
            pl.BlockSpec((1, D_MODEL), lambda i, e: (0, 0)),
            pl.BlockSpec((D_MODEL, 2 * LANES), lambda i, e: (0, 0)),
            pl.BlockSpec((None, D_MODEL, D_FF_EXPERT), lambda i, e: (e, 0, 0)),
            pl.BlockSpec((None, D_MODEL, D_FF_EXPERT), lambda i, e: (e, 0, 0)),
            pl.BlockSpec((None, D_FF_EXPERT, D_MODEL), lambda i, e: (e, 0, 0)),
            pl.BlockSpec((1, D_MODEL), lambda i, e: (0, 0)),
        ],
        out_specs=pl.BlockSpec((tm, D_MODEL), lambda i, e: (i, 0), pipeline_mode=once),
        out_shape=jax.ShapeDtypeStruct((SEQ, D_MODEL), F32),
        scratch_shapes=[
            pltpu.VMEM((tm, D_MODEL), BF16),
            pltpu.VMEM((tm, LANES), F32),
            pltpu.VMEM((tm, LANES), F32),
            pltpu.VMEM((MOE_SUB, LANES, MOE_TM), F32),
            pltpu.VMEM((MOE_SUB, 1, LANES), F32),
            pltpu.VMEM((MOE_PAD, D_MODEL), BF16),
        ],
        compiler_params=_params("parallel", "arbitrary"),
        name="moe",
    )(x, g, w_router, w_gate, w_up, w_down, g_final)


def kernel(x, mix_norm, ffn_norm, final_norm, w_in, ssm_a_re, ssm_a_im, ssm_log_dt, ssm_b_re, ssm_b_im,
           ssm_c_re, ssm_c_im, ssm_d, w_glu, p_attn, p_ssm, w_out, ffn_w_gate, ffn_w_up, ffn_w_down,
           w_router, moe_w_gate, moe_w_up, moe_w_down):
    assert x.shape == (1, SEQ, D_MODEL) and DEPTH == 2
    xs = x.reshape(SEQ, D_MODEL)
    col_scale = jnp.concatenate([jnp.full((SB_WIDTH,), HEAD_DIM ** -0.5 * LOG2E, F32),
                                 jnp.ones((IN_COLS - SB_WIDTH,), F32)])
    for layer in range(DEPTH):
        w_in_b = (w_in[layer] * col_scale).astype(BF16)
        proj, u, vt = _inproj(xs, mix_norm[layer][None], w_in_b)
        o_attn = _attention(proj, vt)
        ssm_w = _ssm_weights(ssm_a_re[layer], ssm_a_im[layer], ssm_log_dt[layer], ssm_b_re[layer],
                             ssm_b_im[layer], ssm_c_re[layer], ssm_c_im[layer], ssm_d[layer])
        y = _ssm(u, *ssm_w)
        xs = _post(xs, o_attn, y, proj, w_glu[layer].astype(BF16), p_attn[layer].astype(BF16),
                   p_ssm[layer].astype(BF16), w_out[layer].astype(BF16))
        i = layer // 2
        if layer % 2 == 0:
            xs = _ffn(xs, ffn_norm[layer][None], ffn_w_gate[i].astype(BF16), ffn_w_up[i].astype(BF16),
                      ffn_w_down[i].astype(BF16))
        else:
            wr = jnp.pad(w_router[i], ((0, 0), (0, LANES - N_EXPERTS)))
            wr_hi = wr.astype(BF16)
            wr = jnp.concatenate([wr_hi, (wr - wr_hi.astype(F32)).astype(BF16)], axis=1)
            xs = _moe(xs, ffn_norm[layer][None], wr, moe_w_gate[i].astype(BF16), moe_w_up[i].astype(BF16),
                      moe_w_down[i].astype(BF16), final_norm[None])
    return xs.reshape(1, SEQ, D_MODEL)
```

```python
import functools

import jax
import jax.numpy as jnp
from jax import lax
from jax.experimental import pallas as pl
from jax.experimental.pallas import tpu as pltpu

F32 = jnp.float32
BF16 = jnp.bfloat16

D_MODEL = 2048
SEQ = 16384
DEPTH = 2
HEAD_DIM = 128
HEADS = 8
SB_WIDTH = HEADS * HEAD_DIM
SSM_WIDTH = 1024
SSM_GROUP = 16
SSM_GROUPS = 64
SSM_STATE = 64
IN_COLS = 3 * SB_WIDTH + SSM_WIDTH + 2 * D_MODEL
D_FF_DENSE = 2 * D_MODEL
N_EXPERTS = 8
D_FF_EXPERT = D_MODEL // 2
RMS_EPS = 1e-6

LANES = 128
SUBLANES = 8
VMEM_LIMIT = 56 * 1024 * 1024

TQ = 256
TK = 256
HP = 4
QB = 2
LOG2E = 1.4426950408889634
UNDERFLOW_LOG2 = 152.0
NORM_MARGIN = 1.01
CHUNK = 8
N_CHUNKS = SEQ // CHUNK
SLAB = LANES
N_SLABS = SSM_WIDTH // SLAB
SLAB_GROUPS = SLAB // SSM_GROUP
SLAB_STATE = SLAB_GROUPS * SSM_STATE
SSM_ROWS = 256


def _params(*sem):
    return pltpu.CompilerParams(dimension_semantics=sem, vmem_limit_bytes=VMEM_LIMIT)


def _rms_rows(x, g):
    ms = jnp.mean(x * x, axis=-1, keepdims=True)
    return x * lax.rsqrt(ms + RMS_EPS) * g


def _sigmoid(x):
    return 1.0 / (1.0 + jnp.exp(-x))


def _inproj_kernel(x_ref, g_ref, w_ref, p_ref, u_ref, vt_ref, h_ref, s_ref):
    j = pl.program_id(1)
    tm = x_ref.shape[0]
    project = lambda h: jnp.dot(h, w_ref[...], preferred_element_type=F32)

    @pl.when(j == 0)
    def _():
        piece = tm // 4
        for rows in (slice(c * piece, (c + 1) * piece) for c in range(4)):
            h = _rms_rows(x_ref[rows, :], g_ref[...]).astype(BF16)
            h_ref[rows, :] = h
            p_ref[rows, :] = project(h).astype(BF16)

    @pl.when(j == 1)
    def _():
        slabs = s_ref.shape[0]
        for b in range(tm // TK):
            rows = slice(b * TK, (b + 1) * TK)
            r = project(h_ref[rows, :])
            p_ref[rows, :] = r.astype(BF16)
            rt = r[:, :SB_WIDTH].T
            for h in range(HEADS):
                vt_ref[h, b] = rt[h * HEAD_DIM:(h + 1) * HEAD_DIM, :].astype(BF16)
            for l in range(slabs):
                s_ref[l, rows, :] = r[:, SB_WIDTH + l * LANES:SB_WIDTH + (l + 1) * LANES]
            out_rows = TK // CHUNK
            u_ref[b * out_rows:(b + 1) * out_rows, :] = jnp.concatenate(
                [s_ref[l, pl.ds(b * TK + c, out_rows, stride=CHUNK), :]
                 for c in range(CHUNK) for l in range(slabs)], axis=1).astype(BF16)

    @pl.when(j >= 2)
    def _():
        p_ref[...] = project(h_ref[...]).astype(BF16)


def _inproj(x, g, w):
    tm, tn = 512, 2048
    assert tn == 2 * SB_WIDTH == SB_WIDTH + SSM_WIDTH and tm % TK == 0 and tm % CHUNK == 0
    return pl.pallas_call(
        _inproj_kernel,
        grid=(SEQ // tm, IN_COLS // tn),
        in_specs=[
            pl.BlockSpec((tm, D_MODEL), lambda i, j: (i, 0)),
            pl.BlockSpec((1, D_MODEL), lambda i, j: (0, 0)),
            pl.BlockSpec((D_MODEL, tn), lambda i, j: (0, j)),
        ],
        out_specs=[
            pl.BlockSpec((tm, tn), lambda i, j: (i, j)),
            pl.BlockSpec((tm // CHUNK, CHUNK * SSM_WIDTH), lambda i, j: (i, 0)),
            pl.BlockSpec((HEADS, tm // TK, HEAD_DIM, TK), lambda i, j: (0, i, 0, 0)),
        ],
        out_shape=[
            jax.ShapeDtypeStruct((SEQ, IN_COLS), BF16),
            jax.ShapeDtypeStruct((N_CHUNKS, CHUNK * SSM_WIDTH), BF16),
            jax.ShapeDtypeStruct((HEADS, SEQ // TK, HEAD_DIM, TK), BF16),
        ],
        scratch_shapes=[pltpu.VMEM((tm, D_MODEL), BF16), pltpu.VMEM((SSM_WIDTH // LANES, tm, LANES), F32)],
        compiler_params=_params("parallel", "arbitrary"),
        name="inproj",
    )(x, g, w)


def _attn_kernel(q_ref, k_ref, vt_ref, o_ref, kmax_ref, carry_ref, acc_ref):
    i = pl.program_id(1)
    hs = lambda h: slice(h * HEAD_DIM, (h + 1) * HEAD_DIM)
    ones = jnp.ones((HEAD_DIM, LANES), BF16)

    def row_norm2(rows_bf16):
        sq = rows_bf16.astype(F32)
        return jnp.dot((sq * sq).astype(BF16), ones, preferred_element_type=F32)

    @pl.when(i == 0)
    def _():
        def body(j, run):
            ks = pl.multiple_of(j * TK, TK)
            new, rows = [], []
            for h in range(HP):
                n2 = jnp.max(row_norm2(k_ref[pl.ds(ks, TK), hs(h)]), axis=0, keepdims=True)[:, :1]
                new.append(jnp.maximum(run[h], n2))
                rows.append(jnp.broadcast_to(jnp.sqrt(new[h]), (1, TQ)))
            kmax_ref[j] = jnp.concatenate(rows, axis=0)
            return tuple(new)

        lax.fori_loop(0, SEQ // TK, body, tuple(jnp.zeros((1, 1), F32) for _ in range(HP)), unroll=4)

    row = lax.broadcasted_iota(jnp.int32, (TK, TQ), 0)
    col = lax.broadcasted_iota(jnp.int32, (TK, TQ), 1)
    upper = jnp.where(col >= row, 1.0, 0.0).astype(BF16)
    causal = row < col

    def block(h, qrows, j, carry, acc, masked):
        ks = pl.multiple_of(j * TK, TK)
        kb = k_ref[pl.ds(ks, TK), hs(h)]
        zt = lax.dot_general(kb, q_ref[qrows, hs(h)], (((1,), (1,)), ((), ())),
                             preferred_element_type=F32)
        sp = jnp.maximum(zt, 0.0) + jnp.log(1.0 + jnp.exp2(-jnp.abs(zt))) * LOG2E
        if masked:
            sp = jnp.where(causal, sp, 0.0)
        rt = jnp.dot(upper, sp.astype(BF16), preferred_element_type=F32)
        w = jnp.exp2(zt - rt - carry)
        if masked:
            w = jnp.where(causal, w, 0.0)
        acc = acc + jnp.dot(vt_ref[h, j], w.astype(BF16), preferred_element_type=F32)
        return carry + rt[0:1, :], acc

    q_blocks = [(qq, i * QB + qq, slice(qq * TQ, (qq + 1) * TQ)) for qq in range(QB)]
    qmax = {}
    for qq, qi, qrows in q_blocks:
        prev = jnp.maximum(qi - 1, 0)
        no_prev = jnp.where(qi == 0, 1e30, 0.0).astype(F32)
        for h in range(HP):
            carry, acc = block(h, qrows, qi, jnp.zeros((1, TQ), F32), jnp.zeros((HEAD_DIM, TQ), F32), True)
            carry, acc = block(h, qrows, prev, carry + no_prev, acc, False)
            carry_ref[qq, h] = carry
            acc_ref[qq, h] = acc
            qf = q_ref[qrows, hs(h)].astype(F32)
            n2 = jnp.max(jnp.sum(qf * qf, axis=1, keepdims=True), axis=0, keepdims=True)
            qmax[qq, h] = jnp.sqrt(n2) * NORM_MARGIN

    def walk_back(qq, qi, qrows):
        def more_needed(j):
            km = kmax_ref[j]
            worst = None
            for h in range(HP):
                slack = carry_ref[qq, h] - (UNDERFLOW_LOG2 + qmax[qq, h] * km[h:h + 1, :])
                worst = slack if worst is None else jnp.minimum(worst, slack)
            return (jnp.min(worst) < 0.0).astype(jnp.int32)

        def cond(c):
            return jnp.logical_and(c[0] >= 0, c[1] > 0)

        def body(c):
            j = c[0]
            for h in range(HP):
                carry, acc = block(h, qrows, j, carry_ref[qq, h], acc_ref[qq, h], False)
                carry_ref[qq, h] = carry
                acc_ref[qq, h] = acc
            return j - 1, more_needed(jnp.maximum(j - 1, 0))

        lax.while_loop(cond, body, (qi - 2, more_needed(jnp.maximum(qi - 2, 0))))

    for qq, qi, qrows in q_blocks:
        walk_back(qq, qi, qrows)
        for h in range(HP):
            o_ref[qrows, hs(h)] = acc_ref[qq, h].T.astype(o_ref.dtype)


def _attention(proj, vt):
    wide = HP * HEAD_DIM
    once = pl.Buffered(1)
    return pl.pallas_call(
        _attn_kernel,
        grid=(HEADS // HP, SEQ // (QB * TQ)),
        in_specs=[
            pl.BlockSpec((QB * TQ, wide), lambda g, i: (i, g)),
            pl.BlockSpec((SEQ, wide), lambda g, i: (0, HEADS // HP + g), pipeline_mode=once),
            pl.BlockSpec((HP, SEQ // TK, HEAD_DIM, TK), lambda g, i: (g, 0, 0, 0), pipeline_mode=once),
        ],
        out_specs=pl.BlockSpec((QB * TQ, wide), lambda g, i: (i, g)),
        out_shape=jax.ShapeDtypeStruct((SEQ, SB_WIDTH), BF16),
        scratch_shapes=[
            pltpu.VMEM((SEQ // TK, HP, TQ), F32),
            pltpu.VMEM((QB, HP, 1, TQ), F32),
            pltpu.VMEM((QB, HP, HEAD_DIM, TQ), F32),
        ],
        compiler_params=_params("arbitrary", "arbitrary"),
        name="attention",
    )(proj, proj, vt)


def _ssm_weights(a_re, a_im, log_dt, b_re, b_im, c_re, c_im, d_skip):
    dt = jnp.exp(log_dt)[:, None]
    d = jnp.arange(CHUNK + 1, dtype=F32)[:, None, None]
    mag = jnp.exp(d * (dt * a_re))
    pw_re = mag * jnp.cos(d * (dt * a_im))
    pw_im = mag * jnp.sin(d * (dt * a_im))
    abar_re, abar_im = pw_re[1], pw_im[1]
    den = a_re * a_re + a_im * a_im
    num_re = abar_re - 1.0
    zoh_re = (num_re * a_re + abar_im * a_im) / den
    zoh_im = (abar_im * a_re - num_re * a_im) / den
    bb_re = zoh_re[..., None] * b_re - zoh_im[..., None] * b_im
    bb_im = zoh_re[..., None] * b_im + zoh_im[..., None] * b_re
    ca_re = c_re[None] * pw_re[:, :, None, :] - c_im[None] * pw_im[:, :, None, :]
    ca_im = c_re[None] * pw_im[:, :, None, :] + c_im[None] * pw_re[:, :, None, :]
    kd = jnp.einsum('dgon,gnp->dgop', ca_re, bb_re) - jnp.einsum('dgon,gnp->dgop', ca_im, bb_im)
    kd = kd.at[0].add(jnp.eye(SSM_GROUP, dtype=F32)[None] * d_skip[:, :, None])
    by_slab = lambda t: t.reshape((N_SLABS, SLAB_GROUPS) + t.shape[1:])
    k_rep = by_slab(kd[:CHUNK].transpose(1, 0, 3, 2)).transpose(0, 2, 3, 1, 4)
    k_rep = k_rep.reshape(N_SLABS, CHUNK, SSM_GROUP, SLAB)
    rev_re, rev_im = pw_re[CHUNK - 1::-1], pw_im[CHUNK - 1::-1]
    wz_re = rev_re[:, :, :, None] * bb_re[None] - rev_im[:, :, :, None] * bb_im[None]
    wz_im = rev_re[:, :, :, None] * bb_im[None] + rev_im[:, :, :, None] * bb_re[None]
    wz = jnp.stack([wz_re, wz_im], axis=0).transpose(2, 1, 4, 0, 3)
    wz_rep = by_slab(wz).transpose(0, 2, 3, 4, 1, 5).reshape(N_SLABS, CHUNK, SSM_GROUP, 2 * SLAB_STATE)
    v = jnp.stack([ca_re[1:], -ca_im[1:]], axis=0).transpose(2, 0, 4, 1, 3)
    v_rep = by_slab(v).transpose(0, 2, 3, 4, 1, 5).reshape(N_SLABS, 2, SSM_STATE, CHUNK * SLAB)
    lam = jnp.stack([pw_re[CHUNK], pw_im[CHUNK]], axis=1)
    lam = by_slab(lam).transpose(0, 2, 1, 3).reshape(N_SLABS, 2, SLAB_STATE)
    wz_mat, t_mat, v_mat = _ssm_expand(k_rep, wz_rep, v_rep)
    return wz_mat, t_mat, v_mat, lam


def _ssm_expand_kernel(k_ref, wz_ref, v_ref, wz_out, t_out, v_out):
    def own_group(rows, log2_per_row, cols, col_period, log2_per_col):
        a = lax.broadcasted_iota(jnp.int32, (rows, cols), 0) >> log2_per_row
        c = (lax.broadcasted_iota(jnp.int32, (rows, cols), 1) & (col_period - 1)) >> log2_per_col
        return a == c

    def expand(vals, mask):
        return jnp.where(mask, jnp.tile(vals, (SLAB_GROUPS, 1)), 0.0).astype(BF16)

    m_t = own_group(SLAB, 4, SLAB, SLAB, 4)
    lag_blocks = [expand(k_ref[d], m_t) for d in range(CHUNK)]
    for j in range(CHUNK):
        for b in range(CHUNK):
            blk = lag_blocks[b - j] if b >= j else jnp.zeros((SLAB, SLAB), BF16)
            t_out[j * SLAB:(j + 1) * SLAB, b * SLAB:(b + 1) * SLAB] = blk
    m_w = own_group(SLAB, 4, 2 * SLAB_STATE, SLAB_STATE, 6)
    for j in range(CHUNK):
        wz_out[j * SLAB:(j + 1) * SLAB, :] = expand(wz_ref[j], m_w)
    m_v = own_group(SLAB_STATE, 6, CHUNK * SLAB, SLAB, 4)
    for r in range(2):
        v_out[r * SLAB_STATE:(r + 1) * SLAB_STATE, :] = expand(v_ref[r], m_v)


def _ssm_expand(k_rep, wz_rep, v_rep):
    assert SSM_GROUP == 1 << 4 and SSM_STATE == 1 << 6
    mat = jax.ShapeDtypeStruct((N_SLABS, CHUNK * SLAB, CHUNK * SLAB), BF16)
    slab3 = lambda shape: pl.BlockSpec((None,) + shape, lambda k: (k, 0, 0))
    return pl.pallas_call(
        _ssm_expand_kernel,
        grid=(N_SLABS,),
        in_specs=[
            pl.BlockSpec((None, CHUNK, SSM_GROUP, SLAB), lambda k: (k, 0, 0, 0)),
            pl.BlockSpec((None, CHUNK, SSM_GROUP, 2 * SLAB_STATE), lambda k: (k, 0, 0, 0)),
            pl.BlockSpec((None, 2, SSM_STATE, CHUNK * SLAB), lambda k: (k, 0, 0, 0)),
        ],
        out_specs=[slab3((CHUNK * SLAB, 2 * SLAB_STATE)), slab3((CHUNK * SLAB, CHUNK * SLAB)),
                   slab3((2 * SLAB_STATE, CHUNK * SLAB))],
        out_shape=[mat, mat, mat],
        compiler_params=_params("parallel"),
        name="ssm_expand",
    )(k_rep, wz_rep, v_rep)


def _cmul(ar, ai, br, bi):
    return ar * br - ai * bi, ar * bi + ai * br


def _ssm_kernel(*refs):
    u_refs = refs[:CHUNK]
    wz_ref, t_ref, v_ref, lam_ref, o_ref, s_ref = refs[CHUNK:]
    ns = SLAB_STATE
    lam1 = (lam_ref[0:1, :], lam_ref[1:2, :])
    lam2 = _cmul(*lam1, *lam1)
    lam4 = _cmul(*lam2, *lam2)
    lam8 = _cmul(*lam4, *lam4)

    def lhs_rows(r0, n):
        return jnp.concatenate([u[pl.ds(r0, n), :] for u in u_refs], axis=1)

    tile_row = lax.broadcasted_iota(jnp.int32, (SUBLANES, ns), 0)
    per_tile = lambda c, shift: jnp.tile(jnp.where(tile_row >= shift, c, 0.0), (SSM_ROWS // SUBLANES, 1))
    levels = [(shift, per_tile(cr, shift), per_tile(ci, shift))
              for shift, (cr, ci) in ((1, lam1), (2, lam2), (4, lam4))]

    def pass1(b, _):
        r0 = pl.multiple_of(b * SSM_ROWS, SSM_ROWS)
        z = jnp.dot(lhs_rows(r0, SSM_ROWS), wz_ref[...], preferred_element_type=F32)
        zr, zi = z[:, :ns], z[:, ns:]
        for shift, cr, ci in levels:
            ar, ai = _cmul(cr, ci, pltpu.roll(zr, shift, axis=0), pltpu.roll(zi, shift, axis=0))
            zr, zi = zr + ar, zi + ai
        s_ref[pl.ds(r0, SSM_ROWS), :ns] = zr
        s_ref[pl.ds(r0, SSM_ROWS), ns:] = zi
        return 0

    lax.fori_loop(0, N_CHUNKS // SSM_ROWS, pass1, 0, unroll=2)

    pow_re, pow_im = [jnp.ones((1, ns), F32)], [jnp.zeros((1, ns), F32)]
    for _ in range(SUBLANES - 1):
        nr, ni = _cmul(pow_re[-1], pow_im[-1], *lam1)
        pow_re.append(nr)
        pow_im.append(ni)
    tab_re = jnp.concatenate(pow_re, axis=0)
    tab_im = jnp.concatenate(pow_im, axis=0)
    sub8 = lax.broadcasted_iota(jnp.int32, (SUBLANES, ns), 0)

    def pass2(t, c):
        cr, ci = c
        r0 = pl.multiple_of(t * SUBLANES, SUBLANES)
        lr = s_ref[pl.ds(r0, SUBLANES), :ns]
        li = s_ref[pl.ds(r0, SUBLANES), ns:]
        er = jnp.where(sub8 >= 1, pltpu.roll(lr, 1, axis=0), 0.0)
        ei = jnp.where(sub8 >= 1, pltpu.roll(li, 1, axis=0), 0.0)
        ar, ai = _cmul(tab_re, tab_im, cr, ci)
        s_ref[pl.ds(r0, SUBLANES), :ns] = er + ar
        s_ref[pl.ds(r0, SUBLANES), ns:] = ei + ai
        nr, ni = _cmul(*lam8, cr, ci)
        return lr[SUBLANES - 1:SUBLANES, :] + nr, li[SUBLANES - 1:SUBLANES, :] + ni

    zero = jnp.zeros((1, ns), F32)
    lax.fori_loop(0, N_CHUNKS // SUBLANES, pass2, (zero, zero), unroll=4)

    for b in range(N_CHUNKS // SSM_ROWS):
        r0 = b * SSM_ROWS
        y = jnp.dot(lhs_rows(r0, SSM_ROWS), t_ref[...], preferred_element_type=F32)
        y = y + jnp.dot(s_ref[pl.ds(r0, SSM_ROWS), :].astype(BF16), v_ref[...],
                        preferred_element_type=F32)
        for j in range(CHUNK):
            o_ref[pl.ds(r0 * CHUNK + j, SSM_ROWS, stride=CHUNK), :] = y[:, j * SLAB:(j + 1) * SLAB]


def _ssm(u2, wz_mat, t_mat, v_mat, lam):
    u_specs = [pl.BlockSpec((N_CHUNKS, SLAB), functools.partial(lambda j, k: (0, j * N_SLABS + k), j))
               for j in range(CHUNK)]
    return pl.pallas_call(
        _ssm_kernel,
        grid=(N_SLABS,),
        in_specs=u_specs + [
            pl.BlockSpec((None, CHUNK * SLAB, 2 * SLAB_STATE), lambda k: (k, 0, 0)),
            pl.BlockSpec((None, CHUNK * SLAB, CHUNK * SLAB), lambda k: (k, 0, 0)),
            pl.BlockSpec((None, 2 * SLAB_STATE, CHUNK * SLAB), lambda k: (k, 0, 0)),
            pl.BlockSpec((None, 2, SLAB_STATE), lambda k: (k, 0, 0)),
        ],
        out_specs=pl.BlockSpec((SEQ, SLAB), lambda k: (0, k), pipeline_mode=pl.Buffered(1)),
        out_shape=jax.ShapeDtypeStruct((SEQ, SSM_WIDTH), F32),
        scratch_shapes=[pltpu.VMEM((N_CHUNKS, 2 * SLAB_STATE), F32)],
        compiler_params=_params("arbitrary"),
        name="ssm",
    )(*([u2] * CHUNK), wz_mat, t_mat, v_mat, lam)


def _gelu_tanh(x):
    c = 0.7978845608028654
    return 0.5 * x * (1.0 + jnp.tanh(c * (x + 0.044715 * (x * x * x))))


def _post_kernel(x_ref, oa_ref, y_ref, ga_ref, gb_ref, wglu_ref, pa_ref, ps_ref, wo_ref, o_ref):
    y = _gelu_tanh(y_ref[...].astype(F32))
    yb = y.astype(BF16)
    glu = jnp.dot(yb, wglu_ref[...], preferred_element_type=F32)
    o_ssm = (y * _sigmoid(glu)).astype(BF16)
    a = jnp.dot(oa_ref[...], pa_ref[...], preferred_element_type=F32)
    b = jnp.dot(o_ssm, ps_ref[...], preferred_element_type=F32)
    merged = _sigmoid(ga_ref[...].astype(F32)) * a + _sigmoid(gb_ref[...].astype(F32)) * b
    o_ref[...] = x_ref[...] + jnp.dot(merged.astype(BF16), wo_ref[...], preferred_element_type=F32)


def _post(x, o_attn, y, proj, w_glu, p_attn, p_ssm, w_out):
    tm = 256
    ga_blk = (3 * SB_WIDTH + SSM_WIDTH) // D_MODEL
    const = lambda shape: pl.BlockSpec(shape, lambda i: (0, 0), pipeline_mode=pl.Buffered(1))
    return pl.pallas_call(
        _post_kernel,
        grid=(SEQ // tm,),
        in_specs=[
            pl.BlockSpec((tm, D_MODEL), lambda i: (i, 0)),
            pl.BlockSpec((tm, SB_WIDTH), lambda i: (i, 0)),
            pl.BlockSpec((tm, SSM_WIDTH), lambda i: (i, 0)),
            pl.BlockSpec((tm, D_MODEL), lambda i: (i, ga_blk)),
            pl.BlockSpec((tm, D_MODEL), lambda i: (i, ga_blk + 1)),
            const((SSM_WIDTH, SSM_WIDTH)),
            const((SB_WIDTH, D_MODEL)),
            const((SSM_WIDTH, D_MODEL)),
            const((D_MODEL, D_MODEL)),
        ],
        out_specs=pl.BlockSpec((tm, D_MODEL), lambda i: (i, 0)),
        out_shape=jax.ShapeDtypeStruct((SEQ, D_MODEL), F32),
        compiler_params=_params("parallel"),
        name="post",
    )(x, o_attn, y, proj, proj, w_glu, p_attn, p_ssm, w_out)


def _ffn_kernel(x_ref, g_ref, wg_ref, wu_ref, wd_ref, o_ref, h_ref):
    f = pl.program_id(1)

    def swiglu(h):
        a = jnp.dot(h, wg_ref[...], preferred_element_type=F32)
        b = jnp.dot(h, wu_ref[...], preferred_element_type=F32)
        t = (a * _sigmoid(a) * b).astype(BF16)
        return jnp.dot(t, wd_ref[...], preferred_element_type=F32)

    @pl.when(f == 0)
    def _():
        piece = x_ref.shape[0] // 4
        for rows in (slice(c * piece, (c + 1) * piece) for c in range(4)):
            x = x_ref[rows, :]
            h = _rms_rows(x, g_ref[...]).astype(BF16)
            h_ref[rows, :] = h
            o_ref[rows, :] = x + swiglu(h)

    @pl.when(f > 0)
    def _():
        o_ref[...] += swiglu(h_ref[...])


def _ffn(x, g, w_gate, w_up, w_down):
    tm, tf = 512, 1024
    return pl.pallas_call(
        _ffn_kernel,
        grid=(SEQ // tm, D_FF_DENSE // tf),
        in_specs=[
            pl.BlockSpec((tm, D_MODEL), lambda i, f: (i, 0)),
            pl.BlockSpec((1, D_MODEL), lambda i, f: (0, 0)),
            pl.BlockSpec((D_MODEL, tf), lambda i, f: (0, f)),
            pl.BlockSpec((D_MODEL, tf), lambda i, f: (0, f)),
            pl.BlockSpec((tf, D_MODEL), lambda i, f: (f, 0)),
        ],
        out_specs=pl.BlockSpec((tm, D_MODEL), lambda i, f: (i, 0)),
        out_shape=jax.ShapeDtypeStruct((SEQ, D_MODEL), F32),
        scratch_shapes=[pltpu.VMEM((tm, D_MODEL), BF16)],
        compiler_params=_params("parallel", "arbitrary"),
        name="ffn",
    )(x, g, w_gate, w_up, w_down)


MOE_TM = 512
MOE_SUB = 2
MOE_ROWS = 144
MOE_PAD = 256


def _moe_kernel(x_ref, g_ref, wr_ref, wg_ref, wu_ref, wd_ref, gf_ref, o_ref,
                h_ref, rw_ref, rk_ref, rkt_ref, cnt_ref, y_ref):
    e = pl.program_id(1)
    tm = MOE_TM
    lane = lax.broadcasted_iota(jnp.int32, (tm, LANES), 1)
    tiles = [slice(u * tm, (u + 1) * tm) for u in range(MOE_SUB)]

    @pl.when(e == 0)
    def _():
        before = (lax.broadcasted_iota(jnp.int32, (tm, tm), 1)
                  < lax.broadcasted_iota(jnp.int32, (tm, tm), 0))
        before = jnp.where(before, 1.0, 0.0).astype(BF16)
        for u, rows in enumerate(tiles):
            x = x_ref[rows, :]
            hf = _rms_rows(x, g_ref[...])
            h_hi = hf.astype(BF16)
            h_ref[rows, :] = h_hi
            o_ref[rows, :] = x
            h_lo = (hf - h_hi.astype(F32)).astype(BF16)
            hi = jnp.dot(h_hi, wr_ref[...], preferred_element_type=F32)
            logits = (hi[:, :LANES] + hi[:, LANES:]
                      + jnp.dot(h_lo, wr_ref[:, :LANES], preferred_element_type=F32))
            lg = jnp.where(lane < N_EXPERTS, logits, -jnp.inf)
            m1 = jnp.max(lg, axis=-1, keepdims=True)
            i1 = jnp.min(jnp.where(lg == m1, lane, LANES), axis=-1, keepdims=True)
            lg2 = jnp.where(lane == i1, -jnp.inf, lg)
            m2 = jnp.max(lg2, axis=-1, keepdims=True)
            i2 = jnp.min(jnp.where(lg2 == m2, lane, LANES), axis=-1, keepdims=True)
            e2 = jnp.exp(m2 - m1)
            g1 = 1.0 / (1.0 + e2)
            g2 = e2 / (1.0 + e2)
            rw_ref[rows, :] = jnp.where(lane == i1, g1, 0.0) + jnp.where(lane == i2, g2, 0.0)
            routed = jnp.logical_or(lane == i1, lane == i2)
            onehot = jnp.where(routed, 1.0, 0.0)
            ahead = jnp.dot(before, onehot.astype(BF16), preferred_element_type=F32)
            rk = jnp.where(routed, ahead, -1.0)
            rk_ref[rows, :] = rk
            rkt_ref[u] = rk.T
            cnt_ref[u] = jnp.sum(onehot, axis=0, keepdims=True)
        y_ref[...] = jnp.zeros_like(y_ref)

    lane1 = lax.broadcasted_iota(jnp.int32, (1, LANES), 1)
    row_id = lax.broadcasted_iota(jnp.int32, (MOE_ROWS, tm), 0).astype(F32)
    col_id = lax.broadcasted_iota(jnp.int32, (tm, MOE_PAD), 1).astype(F32)

    def route_tile(u, rows):
        pick = lambda ref: jnp.sum(jnp.where(lane == e, ref[rows, :], 0.0), axis=-1, keepdims=True)
        rk_col, gate_col = pick(rk_ref), pick(rw_ref)
        rk_row = rkt_ref[u, pl.ds(e, 1), :]
        count = jnp.sum(jnp.where(lane1 == e, cnt_ref[u], 0.0)).astype(jnp.int32)

        def one_pass(s, _):
            base = (s * MOE_ROWS).astype(F32)
            sel = jnp.where(rk_row - base == row_id, 1.0, 0.0).astype(BF16)
            xc = jnp.dot(sel, h_ref[rows, :], preferred_element_type=F32).astype(BF16)
            a = jnp.dot(xc, wg_ref[...], preferred_element_type=F32)
            b = jnp.dot(xc, wu_ref[...], preferred_element_type=F32)
            t = (a * _sigmoid(a) * b).astype(BF16)
            y_ref[:MOE_ROWS, :] = jnp.dot(t, wd_ref[...], preferred_element_type=F32).astype(BF16)
            back = jnp.where(rk_col - base == col_id, 1.0, 0.0).astype(BF16)
            o_ref[rows, :] += gate_col * jnp.dot(back, y_ref[...], preferred_element_type=F32)
            return 0

        lax.fori_loop(0, (count + MOE_ROWS - 1) // MOE_ROWS, one_pass, 0)

    for u, rows in enumerate(tiles):
        route_tile(u, rows)

    @pl.when(e == pl.num_programs(1) - 1)
    def _():
        for rows in tiles:
            o_ref[rows, :] = _rms_rows(o_ref[rows, :], gf_ref[...])


def _moe(x, g, w_router, w_gate, w_up, w_down, g_final):
    tm = MOE_TM * MOE_SUB
    once = pl.Buffered(1)
    return pl.pallas_call(
        _moe_kernel,
        grid=(SEQ // tm, N_EXPERTS),
        in_specs=[
            pl.BlockSpec((tm, D_MODEL), lambda i, e: (i, 0), pipeline_mode=once),
            pl.BlockSpec((1, D_MODEL), lambda i, e: (0, 0)),
            pl.BlockSpec((D_MODEL, 2 * LANES), lambda i, e: (0, 0)),
            pl.BlockSpec((None, D_MODEL, D_FF_EXPERT), lambda i, e: (e, 0, 0)),
            pl.BlockSpec((None, D_MODEL, D_FF_EXPERT), lambda i, e: (e, 0, 0)),
            pl.BlockSpec((None, D_FF_EXPERT, D_MODEL), lambda i, e: (e, 0, 0)),
            pl.BlockSpec((1, D_MODEL), lambda i, e: (0, 0)),
        ],
        out_specs=pl.BlockSpec((tm, D_MODEL), lambda i, e: (i, 0), pipeline_mode=once),
        out_shape=jax.ShapeDtypeStruct((SEQ, D_MODEL), F32),
        scratch_shapes=[
            pltpu.VMEM((tm, D_MODEL), BF16),
            pltpu.VMEM((tm, LANES), F32),
            pltpu.VMEM((tm, LANES), F32),
            pltpu.VMEM((MOE_SUB, LANES, MOE_TM), F32),
            pltpu.VMEM((MOE_SUB, 1, LANES), F32),
            pltpu.VMEM((MOE_PAD, D_MODEL), BF16),
        ],
        compiler_params=_params("parallel", "arbitrary"),
        name="moe",
    )(x, g, w_router, w_gate, w_up, w_down, g_final)


def kernel(x, mix_norm, ffn_norm, final_norm, w_in, ssm_a_re, ssm_a_im, ssm_log_dt, ssm_b_re, ssm_b_im,
           ssm_c_re, ssm_c_im, ssm_d, w_glu, p_attn, p_ssm, w_out, ffn_w_gate, ffn_w_up, ffn_w_down,
           w_router, moe_w_gate, moe_w_up, moe_w_down):
    assert x.shape == (1, SEQ, D_MODEL) and DEPTH == 2
    xs = x.reshape(SEQ, D_MODEL)
    col_scale = jnp.concatenate([jnp.full((SB_WIDTH,), HEAD_DIM ** -0.5 * LOG2E, F32),
                                 jnp.ones((IN_COLS - SB_WIDTH,), F32)])
    for layer in range(DEPTH):
        w_in_b = (w_in[layer] * col_scale).astype(BF16)
        proj, u, vt = _inproj(xs, mix_norm[layer][None], w_in_b)
        o_attn = _attention(proj, vt)
        ssm_w = _ssm_weights(ssm_a_re[layer], ssm_a_im[layer], ssm_log_dt[layer], ssm_b_re[layer],
                             ssm_b_im[layer], ssm_c_re[layer], ssm_c_im[layer], ssm_d[layer])
        y = _ssm(u, *ssm_w)
        xs = _post(xs, o_attn, y, proj, w_glu[layer].astype(BF16), p_attn[layer].astype(BF16),
                   p_ssm[layer].astype(BF16), w_out[layer].astype(BF16))
        i = layer // 2
        if layer % 2 == 0:
            xs = _ffn(xs, ffn_norm[layer][None], ffn_w_gate[i].astype(BF16), ffn_w_up[i].astype(BF16),
                      ffn_w_down[i].astype(BF16))
        else:
            wr = jnp.pad(w_router[i], ((0, 0), (0, LANES - N_EXPERTS)))
            wr_hi = wr.astype(BF16)
            wr = jnp.concatenate([wr_hi, (wr - wr_hi.astype(F32)).astype(BF16)], axis=1)
            xs = _moe(xs, ffn_norm[layer][None], wr, moe_w_gate[i].astype(BF16), moe_w_up[i].astype(BF16),
                      moe_w_down[i].astype(BF16), final_norm[None])
    return xs.reshape(1, SEQ, D_MODEL)
```

```python
import functools

import jax
import jax.numpy as jnp
from jax import lax
from jax.experimental import pallas as pl
from jax.experimental.pallas import tpu as pltpu

F32 = jnp.float32
BF16 = jnp.bfloat16

D_MODEL = 2048
SEQ = 16384
DEPTH = 2
HEAD_DIM = 128
HEADS = 8
SB_WIDTH = HEADS * HEAD_DIM
SSM_WIDTH = 1024
SSM_GROUP = 16
SSM_GROUPS = 64
SSM_STATE = 64
IN_COLS = 3 * SB_WIDTH + SSM_WIDTH + 2 * D_MODEL
D_FF_DENSE = 2 * D_MODEL
N_EXPERTS = 8
D_FF_EXPERT = D_MODEL // 2
RMS_EPS = 1e-6

LANES = 128
SUBLANES = 8
VMEM_LIMIT = 56 * 1024 * 1024

TQ = 256
TK = 256
HP = 4
QB = 2
LOG2E = 1.4426950408889634
UNDERFLOW_LOG2 = 152.0
NORM_MARGIN = 1.01
CHUNK = 8
N_CHUNKS = SEQ // CHUNK
SLAB = LANES
N_SLABS = SSM_WIDTH // SLAB
SLAB_GROUPS = SLAB // SSM_GROUP
SLAB_STATE = SLAB_GROUPS * SSM_STATE
SSM_ROWS = 256


def _params(*sem):
    return pltpu.CompilerParams(dimension_semantics=sem, vmem_limit_bytes=VMEM_LIMIT)


def _rms_rows(x, g):
    ms = jnp.mean(x * x, axis=-1, keepdims=True)
    return x * lax.rsqrt(ms + RMS_EPS) * g


def _sigmoid(x):
    return 1.0 / (1.0 + jnp.exp(-x))


def _inproj_kernel(x_ref, g_ref, w_ref, p_ref, u_ref, vt_ref, h_ref, s_ref):
    j = pl.program_id(1)
    tm = x_ref.shape[0]
    project = lambda h: jnp.dot(h, w_ref[...], preferred_element_type=F32)

    @pl.when(j == 0)
    def _():
        piece = tm // 4
        for rows in (slice(c * piece, (c + 1) * piece) for c in range(4)):
            h = _rms_rows(x_ref[rows, :], g_ref[...]).astype(BF16)
            h_ref[rows, :] = h
            p_ref[rows, :] = project(h).astype(BF16)

    @pl.when(j == 1)
    def _():
        slabs = s_ref.shape[0]
        for b in range(tm // TK):
            rows = slice(b * TK, (b + 1) * TK)
            r = project(h_ref[rows, :])
            p_ref[rows, :] = r.astype(BF16)
            rt = r[:, :SB_WIDTH].T
            for h in range(HEADS):
                vt_ref[h, b] = rt[h * HEAD_DIM:(h + 1) * HEAD_DIM, :].astype(BF16)
            for l in range(slabs):
                s_ref[l, rows, :] = r[:, SB_WIDTH + l * LANES:SB_WIDTH + (l + 1) * LANES]
            out_rows = TK // CHUNK
            u_ref[b * out_rows:(b + 1) * out_rows, :] = jnp.concatenate(
                [s_ref[l, pl.ds(b * TK + c, out_rows, stride=CHUNK), :]
                 for c in range(CHUNK) for l in range(slabs)], axis=1).astype(BF16)

    @pl.when(j >= 2)
    def _():
        p_ref[...] = project(h_ref[...]).astype(BF16)


def _inproj(x, g, w):
    tm, tn = 512, 2048
    assert tn == 2 * SB_WIDTH == SB_WIDTH + SSM_WIDTH and tm % TK == 0 and tm % CHUNK == 0
    return pl.pallas_call(
        _inproj_kernel,
        grid=(SEQ // tm, IN_COLS // tn),
        in_specs=[
            pl.BlockSpec((tm, D_MODEL), lambda i, j: (i, 0)),
            pl.BlockSpec((1, D_MODEL), lambda i, j: (0, 0)),
            pl.BlockSpec((D_MODEL, tn), lambda i, j: (0, j)),
        ],
        out_specs=[
            pl.BlockSpec((tm, tn), lambda i, j: (i, j)),
            pl.BlockSpec((tm // CHUNK, CHUNK * SSM_WIDTH), lambda i, j: (i, 0)),
            pl.BlockSpec((HEADS, tm // TK, HEAD_DIM, TK), lambda i, j: (0, i, 0, 0)),
        ],
        out_shape=[
            jax.ShapeDtypeStruct((SEQ, IN_COLS), BF16),
            jax.ShapeDtypeStruct((N_CHUNKS, CHUNK * SSM_WIDTH), BF16),
            jax.ShapeDtypeStruct((HEADS, SEQ // TK, HEAD_DIM, TK), BF16),
        ],
        scratch_shapes=[pltpu.VMEM((tm, D_MODEL), BF16), pltpu.VMEM((SSM_WIDTH // LANES, tm, LANES), F32)],
        compiler_params=_params("parallel", "arbitrary"),
        name="inproj",
    )(x, g, w)


def _attn_kernel(q_ref, k_ref, vt_ref, o_ref, kmax_ref, carry_ref, acc_ref):
    i = pl.program_id(1)
    hs = lambda h: slice(h * HEAD_DIM, (h + 1) * HEAD_DIM)
    ones = jnp.ones((HEAD_DIM, LANES), BF16)

    def row_norm2(rows_bf16):
        sq = rows_bf16.astype(F32)
        return jnp.dot((sq * sq).astype(BF16), ones, preferred_element_type=F32)

    @pl.when(i == 0)
    def _():
        def body(j, run):
            ks = pl.multiple_of(j * TK, TK)
            new, rows = [], []
            for h in range(HP):
                n2 = jnp.max(row_norm2(k_ref[pl.ds(ks, TK), hs(h)]), axis=0, keepdims=True)[:, :1]
                new.append(jnp.maximum(run[h], n2))
                rows.append(jnp.broadcast_to(jnp.sqrt(new[h]), (1, TQ)))
            kmax_ref[j] = jnp.concatenate(rows, axis=0)
            return tuple(new)

        lax.fori_loop(0, SEQ // TK, body, tuple(jnp.zeros((1, 1), F32) for _ in range(HP)), unroll=4)

    row = lax.broadcasted_iota(jnp.int32, (TK, TQ), 0)
    col = lax.broadcasted_iota(jnp.int32, (TK, TQ), 1)
    upper = jnp.where(col >= row, 1.0, 0.0).astype(BF16)
    causal = row < col

    def block(h, qrows, j, carry, acc, masked):
        ks = pl.multiple_of(j * TK, TK)
        kb = k_ref[pl.ds(ks, TK), hs(h)]
        zt = lax.dot_general(kb, q_ref[qrows, hs(h)], (((1,), (1,)), ((), ())),
                             preferred_element_type=F32)
        sp = jnp.maximum(zt, 0.0) + jnp.log(1.0 + jnp.exp2(-jnp.abs(zt))) * LOG2E
        if masked:
            sp = jnp.where(causal, sp, 0.0)
        rt = jnp.dot(upper, sp.astype(BF16), preferred_element_type=F32)
        w = jnp.exp2(zt - rt - carry)
        if masked:
            w = jnp.where(causal, w, 0.0)
        acc = acc + jnp.dot(vt_ref[h, j], w.astype(BF16), preferred_element_type=F32)
        return carry + rt[0:1, :], acc

    q_blocks = [(qq, i * QB + qq, slice(qq * TQ, (qq + 1) * TQ)) for qq in range(QB)]
    qmax = {}
    for qq, qi, qrows in q_blocks:
        prev = jnp.maximum(qi - 1, 0)
        no_prev = jnp.where(qi == 0, 1e30, 0.0).astype(F32)
        for h in range(HP):
            carry, acc = block(h, qrows, qi, jnp.zeros((1, TQ), F32), jnp.zeros((HEAD_DIM, TQ), F32), True)
            carry, acc = block(h, qrows, prev, carry + no_prev, acc, False)
            carry_ref[qq, h] = carry
            acc_ref[qq, h] = acc
            qf = q_ref[qrows, hs(h)].astype(F32)
            n2 = jnp.max(jnp.sum(qf * qf, axis=1, keepdims=True), axis=0, keepdims=True)
            qmax[qq, h] = jnp.sqrt(n2) * NORM_MARGIN

    def walk_back(qq, qi, qrows):
        def more_needed(j):
            km = kmax_ref[j]
            worst = None
            for h in range(HP):
                slack = carry_ref[qq, h] - (UNDERFLOW_LOG2 + qmax[qq, h] * km[h:h + 1, :])
                worst = slack if worst is None else jnp.minimum(worst, slack)
            return (jnp.min(worst) < 0.0).astype(jnp.int32)

        def cond(c):
            return jnp.logical_and(c[0] >= 0, c[1] > 0)

        def body(c):
            j = c[0]
            for h in range(HP):
                carry, acc = block(h, qrows, j, carry_ref[qq, h], acc_ref[qq, h], False)
                carry_ref[qq, h] = carry
                acc_ref[qq, h] = acc
            return j - 1, more_needed(jnp.maximum(j - 1, 0))

        lax.while_loop(cond, body, (qi - 2, more_needed(jnp.maximum(qi - 2, 0))))

    for qq, qi, qrows in q_blocks:
        walk_back(qq, qi, qrows)
        for h in range(HP):
            o_ref[qrows, hs(h)] = acc_ref[qq, h].T.astype(o_ref.dtype)


def _attention(proj, vt):
    wide = HP * HEAD_DIM
    once = pl.Buffered(1)
    return pl.pallas_call(
        _attn_kernel,
        grid=(HEADS // HP, SEQ // (QB * TQ)),
        in_specs=[
            pl.BlockSpec((QB * TQ, wide), lambda g, i: (i, g)),
            pl.BlockSpec((SEQ, wide), lambda g, i: (0, HEADS // HP + g), pipeline_mode=once),
            pl.BlockSpec((HP, SEQ // TK, HEAD_DIM, TK), lambda g, i: (g, 0, 0, 0), pipeline_mode=once),
        ],
        out_specs=pl.BlockSpec((QB * TQ, wide), lambda g, i: (i, g)),
        out_shape=jax.ShapeDtypeStruct((SEQ, SB_WIDTH), BF16),
        scratch_shapes=[
            pltpu.VMEM((SEQ // TK, HP, TQ), F32),
            pltpu.VMEM((QB, HP, 1, TQ), F32),
            pltpu.VMEM((QB, HP, HEAD_DIM, TQ), F32),
        ],
        compiler_params=_params("arbitrary", "arbitrary"),
        name="attention",
    )(proj, proj, vt)


def _ssm_weights(a_re, a_im, log_dt, b_re, b_im, c_re, c_im, d_skip):
    dt = jnp.exp(log_dt)[:, None]
    d = jnp.arange(CHUNK + 1, dtype=F32)[:, None, None]
    mag = jnp.exp(d * (dt * a_re))
    pw_re = mag * jnp.cos(d * (dt * a_im))
    pw_im = mag * jnp.sin(d * (dt * a_im))
    abar_re, abar_im = pw_re[1], pw_im[1]
    den = a_re * a_re + a_im * a_im
    num_re = abar_re - 1.0
    zoh_re = (num_re * a_re + abar_im * a_im) / den
    zoh_im = (abar_im * a_re - num_re * a_im) / den
    bb_re = zoh_re[..., None] * b_re - zoh_im[..., None] * b_im
    bb_im = zoh_re[..., None] * b_im + zoh_im[..., None] * b_re
    ca_re = c_re[None] * pw_re[:, :, None, :] - c_im[None] * pw_im[:, :, None, :]
    ca_im = c_re[None] * pw_im[:, :, None, :] + c_im[None] * pw_re[:, :, None, :]
    kd = jnp.einsum('dgon,gnp->dgop', ca_re, bb_re) - jnp.einsum('dgon,gnp->dgop', ca_im, bb_im)
    kd = kd.at[0].add(jnp.eye(SSM_GROUP, dtype=F32)[None] * d_skip[:, :, None])
    by_slab = lambda t: t.reshape((N_SLABS, SLAB_GROUPS) + t.shape[1:])
    k_rep = by_slab(kd[:CHUNK].transpose(1, 0, 3, 2)).transpose(0, 2, 3, 1, 4)
    k_rep = k_rep.reshape(N_SLABS, CHUNK, SSM_GROUP, SLAB)
    rev_re, rev_im = pw_re[CHUNK - 1::-1], pw_im[CHUNK - 1::-1]
    wz_re = rev_re[:, :, :, None] * bb_re[None] - rev_im[:, :, :, None] * bb_im[None]
    wz_im = rev_re[:, :, :, None] * bb_im[None] + rev_im[:, :, :, None] * bb_re[None]
    wz = jnp.stack([wz_re, wz_im], axis=0).transpose(2, 1, 4, 0, 3)
    wz_rep = by_slab(wz).transpose(0, 2, 3, 4, 1, 5).reshape(N_SLABS, CHUNK, SSM_GROUP, 2 * SLAB_STATE)
    v = jnp.stack([ca_re[1:], -ca_im[1:]], axis=0).transpose(2, 0, 4, 1, 3)
    v_rep = by_slab(v).transpose(0, 2, 3, 4, 1, 5).reshape(N_SLABS, 2, SSM_STATE, CHUNK * SLAB)
    lam = jnp.stack([pw_re[CHUNK], pw_im[CHUNK]], axis=1)
    lam = by_slab(lam).transpose(0, 2, 1, 3).reshape(N_SLABS, 2, SLAB_STATE)
    wz_mat, t_mat, v_mat = _ssm_expand(k_rep, wz_rep, v_rep)
    return wz_mat, t_mat, v_mat, lam


def _ssm_expand_kernel(k_ref, wz_ref, v_ref, wz_out, t_out, v_out):
    def own_group(rows, log2_per_row, cols, col_period, log2_per_col):
        a = lax.broadcasted_iota(jnp.int32, (rows, cols), 0) >> log2_per_row
        c = (lax.broadcasted_iota(jnp.int32, (rows, cols), 1) & (col_period - 1)) >> log2_per_col
        return a == c

    def expand(vals, mask):
        return jnp.where(mask, jnp.tile(vals, (SLAB_GROUPS, 1)), 0.0).astype(BF16)

    m_t = own_group(SLAB, 4, SLAB, SLAB, 4)
    lag_blocks = [expand(k_ref[d], m_t) for d in range(CHUNK)]
    for j in range(CHUNK):
        for b in range(CHUNK):
            blk = lag_blocks[b - j] if b >= j else jnp.zeros((SLAB, SLAB), BF16)
            t_out[j * SLAB:(j + 1) * SLAB, b * SLAB:(b + 1) * SLAB] = blk
    m_w = own_group(SLAB, 4, 2 * SLAB_STATE, SLAB_STATE, 6)
    for j in range(CHUNK):
        wz_out[j * SLAB:(j + 1) * SLAB, :] = expand(wz_ref[j], m_w)
    m_v = own_group(SLAB_STATE, 6, CHUNK * SLAB, SLAB, 4)
    for r in range(2):
        v_out[r * SLAB_STATE:(r + 1) * SLAB_STATE, :] = expand(v_ref[r], m_v)


def _ssm_expand(k_rep, wz_rep, v_rep):
    assert SSM_GROUP == 1 << 4 and SSM_STATE == 1 << 6
    mat = jax.ShapeDtypeStruct((N_SLABS, CHUNK * SLAB, CHUNK * SLAB), BF16)
    slab3 = lambda shape: pl.BlockSpec((None,) + shape, lambda k: (k, 0, 0))
    return pl.pallas_call(
        _ssm_expand_kernel,
        grid=(N_SLABS,),
        in_specs=[
            pl.BlockSpec((None, CHUNK, SSM_GROUP, SLAB), lambda k: (k, 0, 0, 0)),
            pl.BlockSpec((None, CHUNK, SSM_GROUP, 2 * SLAB_STATE), lambda k: (k, 0, 0, 0)),
            pl.BlockSpec((None, 2, SSM_STATE, CHUNK * SLAB), lambda k: (k, 0, 0, 0)),
        ],
        out_specs=[slab3((CHUNK * SLAB, 2 * SLAB_STATE)), slab3((CHUNK * SLAB, CHUNK * SLAB)),
                   slab3((2 * SLAB_STATE, CHUNK * SLAB))],
        out_shape=[mat, mat, mat],
        compiler_params=_params("parallel"),
        name="ssm_expand",
    )(k_rep, wz_rep, v_rep)


def _cmul(ar, ai, br, bi):
    return ar * br - ai * bi, ar * bi + ai * br


def _ssm_kernel(*refs):
    u_refs = refs[:CHUNK]
    wz_ref, t_ref, v_ref, lam_ref, o_ref, s_ref = refs[CHUNK:]
    ns = SLAB_STATE
    lam1 = (lam_ref[0:1, :], lam_ref[1:2, :])
    lam2 = _cmul(*lam1, *lam1)
    lam4 = _cmul(*lam2, *lam2)
    lam8 = _cmul(*lam4, *lam4)

    def lhs_rows(r0, n):
        return jnp.concatenate([u[pl.ds(r0, n), :] for u in u_refs], axis=1)

    tile_row = lax.broadcasted_iota(jnp.int32, (SUBLANES, ns), 0)
    per_tile = lambda c, shift: jnp.tile(jnp.where(tile_row >= shift, c, 0.0), (SSM_ROWS // SUBLANES, 1))
    levels = [(shift, per_tile(cr, shift), per_tile(ci, shift))
              for shift, (cr, ci) in ((1, lam1), (2, lam2), (4, lam4))]

    def pass1(b, _):
        r0 = pl.multiple_of(b * SSM_ROWS, SSM_ROWS)
        z = jnp.dot(lhs_rows(r0, SSM_ROWS), wz_ref[...], preferred_element_type=F32)
        zr, zi = z[:, :ns], z[:, ns:]
        for shift, cr, ci in levels:
            ar, ai = _cmul(cr, ci, pltpu.roll(zr, shift, axis=0), pltpu.roll(zi, shift, axis=0))
            zr, zi = zr + ar, zi + ai
        s_ref[pl.ds(r0, SSM_ROWS), :ns] = zr
        s_ref[pl.ds(r0, SSM_ROWS), ns:] = zi
        return 0

    lax.fori_loop(0, N_CHUNKS // SSM_ROWS, pass1, 0, unroll=2)

    pow_re, pow_im = [jnp.ones((1, ns), F32)], [jnp.zeros((1, ns), F32)]
    for _ in range(SUBLANES - 1):
        nr, ni = _cmul(pow_re[-1], pow_im[-1], *lam1)
        pow_re.append(nr)
        pow_im.append(ni)
    tab_re = jnp.concatenate(pow_re, axis=0)
    tab_im = jnp.concatenate(pow_im, axis=0)
    sub8 = lax.broadcasted_iota(jnp.int32, (SUBLANES, ns), 0)

    def pass2(t, c):
        cr, ci = c
        r0 = pl.multiple_of(t * SUBLANES, SUBLANES)
        lr = s_ref[pl.ds(r0, SUBLANES), :ns]
        li = s_ref[pl.ds(r0, SUBLANES), ns:]
        er = jnp.where(sub8 >= 1, pltpu.roll(lr, 1, axis=0), 0.0)
        ei = jnp.where(sub8 >= 1, pltpu.roll(li, 1, axis=0), 0.0)
        ar, ai = _cmul(tab_re, tab_im, cr, ci)
        s_ref[pl.ds(r0, SUBLANES), :ns] = er + ar
        s_ref[pl.ds(r0, SUBLANES), ns:] = ei + ai
        nr, ni = _cmul(*lam8, cr, ci)
        return lr[SUBLANES - 1:SUBLANES, :] + nr, li[SUBLANES - 1:SUBLANES, :] + ni

    zero = jnp.zeros((1, ns), F32)
    lax.fori_loop(0, N_CHUNKS // SUBLANES, pass2, (zero, zero), unroll=4)

    for b in range(N_CHUNKS // SSM_ROWS):
        r0 = b * SSM_ROWS
        lhs = lhs_rows(r0, SSM_ROWS)
        state = s_ref[pl.ds(r0, SSM_ROWS), :].astype(BF16)
        for j0 in range(0, CHUNK, 2):
            cols = slice(j0 * SLAB, (j0 + 2) * SLAB)
            k = (j0 + 2) * SLAB
            y = jnp.dot(lhs[:, :k], t_ref[:k, cols], preferred_element_type=F32)
            y = y + jnp.dot(state, v_ref[:, cols], preferred_element_type=F32)
            for j in range(2):
                o_ref[pl.ds(r0 * CHUNK + j0 + j, SSM_ROWS, stride=CHUNK), :] = y[:, j * SLAB:(j + 1) * SLAB]


def _ssm(u2, wz_mat, t_mat, v_mat, lam):
    u_specs = [pl.BlockSpec((N_CHUNKS, SLAB), functools.partial(lambda j, k: (0, j * N_SLABS + k), j))
               for j in range(CHUNK)]
    return pl.pallas_call(
        _ssm_kernel,
        grid=(N_SLABS,),
        in_specs=u_specs + [
            pl.BlockSpec((None, CHUNK * SLAB, 2 * SLAB_STATE), lambda k: (k, 0, 0)),
            pl.BlockSpec((None, CHUNK * SLAB, CHUNK * SLAB), lambda k: (k, 0, 0)),
            pl.BlockSpec((None, 2 * SLAB_STATE, CHUNK * SLAB), lambda k: (k, 0, 0)),
            pl.BlockSpec((None, 2, SLAB_STATE), lambda k: (k, 0, 0)),
        ],
        out_specs=pl.BlockSpec((SEQ, SLAB), lambda k: (0, k), pipeline_mode=pl.Buffered(1)),
        out_shape=jax.ShapeDtypeStruct((SEQ, SSM_WIDTH), F32),
        scratch_shapes=[pltpu.VMEM((N_CHUNKS, 2 * SLAB_STATE), F32)],
        compiler_params=_params("arbitrary"),
        name="ssm",
    )(*([u2] * CHUNK), wz_mat, t_mat, v_mat, lam)


def _gelu_tanh(x):
    c = 0.7978845608028654
    return 0.5 * x * (1.0 + jnp.tanh(c * (x + 0.044715 * (x * x * x))))


def _post_kernel(x_ref, oa_ref, y_ref, ga_ref, gb_ref, wglu_ref, pa_ref, ps_ref, wo_ref, o_ref):
    y = _gelu_tanh(y_ref[...].astype(F32))
    yb = y.astype(BF16)
    glu = jnp.dot(yb, wglu_ref[...], preferred_element_type=F32)
    o_ssm = (y * _sigmoid(glu)).astype(BF16)
    a = jnp.dot(oa_ref[...], pa_ref[...], preferred_element_type=F32)
    b = jnp.dot(o_ssm, ps_ref[...], preferred_element_type=F32)
    merged = _sigmoid(ga_ref[...].astype(F32)) * a + _sigmoid(gb_ref[...].astype(F32)) * b
    o_ref[...] = x_ref[...] + jnp.dot(merged.astype(BF16), wo_ref[...], preferred_element_type=F32)


def _post(x, o_attn, y, proj, w_glu, p_attn, p_ssm, w_out):
    tm = 256
    ga_blk = (3 * SB_WIDTH + SSM_WIDTH) // D_MODEL
    const = lambda shape: pl.BlockSpec(shape, lambda i: (0, 0), pipeline_mode=pl.Buffered(1))
    return pl.pallas_call(
        _post_kernel,
        grid=(SEQ // tm,),
        in_specs=[
            pl.BlockSpec((tm, D_MODEL), lambda i: (i, 0)),
            pl.BlockSpec((tm, SB_WIDTH), lambda i: (i, 0)),
            pl.BlockSpec((tm, SSM_WIDTH), lambda i: (i, 0)),
            pl.BlockSpec((tm, D_MODEL), lambda i: (i, ga_blk)),
            pl.BlockSpec((tm, D_MODEL), lambda i: (i, ga_blk + 1)),
            const((SSM_WIDTH, SSM_WIDTH)),
            const((SB_WIDTH, D_MODEL)),
            const((SSM_WIDTH, D_MODEL)),
            const((D_MODEL, D_MODEL)),
        ],
        out_specs=pl.BlockSpec((tm, D_MODEL), lambda i: (i, 0)),
        out_shape=jax.ShapeDtypeStruct((SEQ, D_MODEL), F32),
        compiler_params=_params("parallel"),
        name="post",
    )(x, o_attn, y, proj, proj, w_glu, p_attn, p_ssm, w_out)


def _ffn_kernel(x_ref, g_ref, wg_ref, wu_ref, wd_ref, o_ref, h_ref):
    f = pl.program_id(1)

    def swiglu(h):
        a = jnp.dot(h, wg_ref[...], preferred_element_type=F32)
        b = jnp.dot(h, wu_ref[...], preferred_element_type=F32)
        t = (a * _sigmoid(a) * b).astype(BF16)
        return jnp.dot(t, wd_ref[...], preferred_element_type=F32)

    @pl.when(f == 0)
    def _():
        piece = x_ref.shape[0] // 4
        for rows in (slice(c * piece, (c + 1) * piece) for c in range(4)):
            x = x_ref[rows, :]
            h = _rms_rows(x, g_ref[...]).astype(BF16)
            h_ref[rows, :] = h
            o_ref[rows, :] = x + swiglu(h)

    @pl.when(f > 0)
    def _():
        o_ref[...] += swiglu(h_ref[...])


def _ffn(x, g, w_gate, w_up, w_down):
    tm, tf = 512, 1024
    return pl.pallas_call(
        _ffn_kernel,
        grid=(SEQ // tm, D_FF_DENSE // tf),
        in_specs=[
            pl.BlockSpec((tm, D_MODEL), lambda i, f: (i, 0)),
            pl.BlockSpec((1, D_MODEL), lambda i, f: (0, 0)),
            pl.BlockSpec((D_MODEL, tf), lambda i, f: (0, f)),
            pl.BlockSpec((D_MODEL, tf), lambda i, f: (0, f)),
            pl.BlockSpec((tf, D_MODEL), lambda i, f: (f, 0)),
        ],
        out_specs=pl.BlockSpec((tm, D_MODEL), lambda i, f: (i, 0)),
        out_shape=jax.ShapeDtypeStruct((SEQ, D_MODEL), F32),
        scratch_shapes=[pltpu.VMEM((tm, D_MODEL), BF16)],
        compiler_params=_params("parallel", "arbitrary"),
        name="ffn",
    )(x, g, w_gate, w_up, w_down)


MOE_TM = 512
MOE_SUB = 2
MOE_ROWS = 160
MOE_PAD = 256


def _moe_kernel(x_ref, g_ref, wr_ref, wg_ref, wu_ref, wd_ref, gf_ref, o_ref,
                h_ref, rw_ref, rk_ref, rkt_ref, cnt_ref, y_ref):
    e = pl.program_id(1)
    tm = MOE_TM
    lane = lax.broadcasted_iota(jnp.int32, (tm, LANES), 1)
    tiles = [slice(u * tm, (u + 1) * tm) for u in range(MOE_SUB)]

    @pl.when(e == 0)
    def _():
        before = (lax.broadcasted_iota(jnp.int32, (tm, tm), 1)
                  < lax.broadcasted_iota(jnp.int32, (tm, tm), 0))
        before = jnp.where(before, 1.0, 0.0).astype(BF16)
        for u, rows in enumerate(tiles):
            x = x_ref[rows, :]
            hf = _rms_rows(x, g_ref[...])
            h_hi = hf.astype(BF16)
            h_ref[rows, :] = h_hi
            o_ref[rows, :] = x
            h_lo = (hf - h_hi.astype(F32)).astype(BF16)
            hi = jnp.dot(h_hi, wr_ref[...], preferred_element_type=F32)
            logits = (hi[:, :LANES] + hi[:, LANES:]
                      + jnp.dot(h_lo, wr_ref[:, :LANES], preferred_element_type=F32))
            lg = jnp.where(lane < N_EXPERTS, logits, -jnp.inf)
            m1 = jnp.max(lg, axis=-1, keepdims=True)
            i1 = jnp.min(jnp.where(lg == m1, lane, LANES), axis=-1, keepdims=True)
            lg2 = jnp.where(lane == i1, -jnp.inf, lg)
            m2 = jnp.max(lg2, axis=-1, keepdims=True)
            i2 = jnp.min(jnp.where(lg2 == m2, lane, LANES), axis=-1, keepdims=True)
            e2 = jnp.exp(m2 - m1)
            g1 = 1.0 / (1.0 + e2)
            g2 = e2 / (1.0 + e2)
            rw_ref[rows, :] = jnp.where(lane == i1, g1, 0.0) + jnp.where(lane == i2, g2, 0.0)
            routed = jnp.logical_or(lane == i1, lane == i2)
            onehot = jnp.where(routed, 1.0, 0.0)
            ahead = jnp.dot(before, onehot.astype(BF16), preferred_element_type=F32)
            rk = jnp.where(routed, ahead, -1.0)
            rk_ref[rows, :] = rk
            rkt_ref[u] = rk.T
            cnt_ref[u] = jnp.sum(onehot, axis=0, keepdims=True)
        y_ref[...] = jnp.zeros_like(y_ref)

    lane1 = lax.broadcasted_iota(jnp.int32, (1, LANES), 1)
    row_id = lax.broadcasted_iota(jnp.int32, (MOE_ROWS, tm), 0).astype(F32)
    col_id = lax.broadcasted_iota(jnp.int32, (tm, MOE_PAD), 1).astype(F32)

    def route_tile(u, rows):
        pick = lambda ref: jnp.sum(jnp.where(lane == e, ref[rows, :], 0.0), axis=-1, keepdims=True)
        rk_col, gate_col = pick(rk_ref), pick(rw_ref)
        rk_row = rkt_ref[u, pl.ds(e, 1), :]
        count = jnp.sum(jnp.where(lane1 == e, cnt_ref[u], 0.0)).astype(jnp.int32)

        def one_pass(s, _):
            base = (s * MOE_ROWS).astype(F32)
            sel = jnp.where(rk_row - base == row_id, 1.0, 0.0).astype(BF16)
            xc = jnp.dot(sel, h_ref[rows, :], preferred_element_type=F32).astype(BF16)
            a = jnp.dot(xc, wg_ref[...], preferred_element_type=F32)
            b = jnp.dot(xc, wu_ref[...], preferred_element_type=F32)
            t = (a * _sigmoid(a) * b).astype(BF16)
            y_ref[:MOE_ROWS, :] = jnp.dot(t, wd_ref[...], preferred_element_type=F32).astype(BF16)
            back = jnp.where(rk_col - base == col_id, 1.0, 0.0).astype(BF16)
            o_ref[rows, :] += gate_col * jnp.dot(back, y_ref[...], preferred_element_type=F32)
            return 0

        lax.fori_loop(0, (count + MOE_ROWS - 1) // MOE_ROWS, one_pass, 0)

    for u, rows in enumerate(tiles):
        route_tile(u, rows)

    @pl.when(e == pl.num_programs(1) - 1)
    def _():
        for rows in tiles:
            o_ref[rows, :] = _rms_rows(o_ref[rows, :], gf_ref[...])


def _moe(x, g, w_router, w_gate, w_up, w_down, g_final):
    tm = MOE_TM * MOE_SUB
    once = pl.Buffered(1)
    return pl.pallas_call(
        _moe_kernel,
        grid=(SEQ // tm, N_EXPERTS),
        in_specs=[
            pl.BlockSpec((tm, D_MODEL), lambda i, e: (i, 0), pipeline_mode=once),
            pl.BlockSpec((1, D_MODEL), lambda i, e: (0, 0)),
            pl.BlockSpec((D_MODEL, 2 * LANES), lambda i, e: (0, 0)),
            pl.BlockSpec((None, D_MODEL, D_FF_EXPERT), lambda i, e: (e, 0, 0)),
            pl.BlockSpec((None, D_MODEL, D_FF_EXPERT), lambda i, e: (e, 0, 0)),
            pl.BlockSpec((None, D_FF_EXPERT, D_MODEL), lambda i, e: (e, 0, 0)),
            pl.BlockSpec((1, D_MODEL), lambda i, e: (0, 0)),
        ],
        out_specs=pl.BlockSpec((tm, D_MODEL), lambda i, e: (i, 0), pipeline_mode=once),
        out_shape=jax.ShapeDtypeStruct((SEQ, D_MODEL), F32),
        scratch_shapes=[
            pltpu.VMEM((tm, D_MODEL), BF16),
            pltpu.VMEM((tm, LANES), F32),
            pltpu.VMEM((tm, LANES), F32),
            pltpu.VMEM((MOE_SUB, LANES, MOE_TM), F32),
            pltpu.VMEM((MOE_SUB, 1, LANES), F32),
            pltpu.VMEM((MOE_PAD, D_MODEL), BF16),
        ],
        compiler_params=_params("parallel", "arbitrary"),
        name="moe",
    )(x, g, w_router, w_gate, w_up, w_down, g_final)


def kernel(x, mix_norm, ffn_norm, final_norm, w_in, ssm_a_re, ssm_a_im, ssm_log_dt, ssm_b_re, ssm_b_im,
           ssm_c_re, ssm_c_im, ssm_d, w_glu, p_attn, p_ssm, w_out, ffn_w_gate, ffn_w_up, ffn_w_down,
           w_router, moe_w_gate, moe_w_up, moe_w_down):
    assert x.shape == (1, SEQ, D_MODEL) and DEPTH == 2
    xs = x.reshape(SEQ, D_MODEL)
    col_scale = jnp.concatenate([jnp.full((SB_WIDTH,), HEAD_DIM ** -0.5 * LOG2E, F32),
                                 jnp.ones((IN_COLS - SB_WIDTH,), F32)])
    for layer in range(DEPTH):
        w_in_b = (w_in[layer] * col_scale).astype(BF16)
        proj, u, vt = _inproj(xs, mix_norm[layer][None], w_in_b)
        o_attn = _attention(proj, vt)
        ssm_w = _ssm_weights(ssm_a_re[layer], ssm_a_im[layer], ssm_log_dt[layer], ssm_b_re[layer],
                             ssm_b_im[layer], ssm_c_re[layer], ssm_c_im[layer], ssm_d[layer])
        y = _ssm(u, *ssm_w)
        xs = _post(xs, o_attn, y, proj, w_glu[layer].astype(BF16), p_attn[layer].astype(BF16),
                   p_ssm[layer].astype(BF16), w_out[layer].astype(BF16))
        i = layer // 2
        if layer % 2 == 0:
            xs = _ffn(xs, ffn_norm[layer][None], ffn_w_gate[i].astype(BF16), ffn_w_up[i].astype(BF16),
                      ffn_w_down[i].astype(BF16))
        else:
            wr = jnp.pad(w_router[i], ((0, 0), (0, LANES - N_EXPERTS)))
            wr_hi = wr.astype(BF16)
            wr = jnp.concatenate([wr_hi, (wr - wr_hi.astype(F32)).astype(BF16)], axis=1)
            xs = _moe(xs, ffn_norm[layer][None], wr, moe_w_gate[i].astype(BF16), moe_w_up[i].astype(BF16),
                      moe_w_down[i].astype(BF16), final_norm[None])
    return xs.reshape(1, SEQ, D_MODEL)
```

```python
import functools

import jax
import jax.numpy as jnp
from jax import lax
from jax.experimental import pallas as pl
from jax.experimental.pallas import tpu as pltpu

F32 = jnp.float32
BF16 = jnp.bfloat16

D_MODEL = 2048
SEQ = 16384
DEPTH = 2
HEAD_DIM = 128
HEADS = 8
SB_WIDTH = HEADS * HEAD_DIM
SSM_WIDTH = 1024
SSM_GROUP = 16
SSM_GROUPS = 64
SSM_STATE = 64
IN_COLS = 3 * SB_WIDTH + SSM_WIDTH + 2 * D_MODEL
D_FF_DENSE = 2 * D_MODEL
N_EXPERTS = 8
D_FF_EXPERT = D_MODEL // 2
RMS_EPS = 1e-6

LANES = 128
SUBLANES = 8
VMEM_LIMIT = 56 * 1024 * 1024

TQ = 256
TK = 256
HP = 4
QB = 4
LOG2E = 1.4426950408889634
UNDERFLOW_LOG2 = 152.0
NORM_MARGIN = 1.01
CHUNK = 8
N_CHUNKS = SEQ // CHUNK
SLAB = LANES
N_SLABS = SSM_WIDTH // SLAB
SLAB_GROUPS = SLAB // SSM_GROUP
SLAB_STATE = SLAB_GROUPS * SSM_STATE
SSM_ROWS = 256


def _params(*sem):
    return pltpu.CompilerParams(dimension_semantics=sem, vmem_limit_bytes=VMEM_LIMIT)


def _rms_rows(x, g):
    ms = jnp.mean(x * x, axis=-1, keepdims=True)
    return x * lax.rsqrt(ms + RMS_EPS) * g


def _sigmoid(x):
    return 1.0 / (1.0 + jnp.exp(-x))


def _inproj_kernel(x_ref, g_ref, w_ref, p_ref, u_ref, vt_ref, h_ref, s_ref):
    j = pl.program_id(1)
    tm = x_ref.shape[0]
    project = lambda h: jnp.dot(h, w_ref[...], preferred_element_type=F32)

    @pl.when(j == 0)
    def _():
        piece = tm // 4
        for rows in (slice(c * piece, (c + 1) * piece) for c in range(4)):
            h = _rms_rows(x_ref[rows, :], g_ref[...]).astype(BF16)
            h_ref[rows, :] = h
            p_ref[rows, :] = project(h).astype(BF16)

    @pl.when(j == 1)
    def _():
        slabs = s_ref.shape[0]
        for b in range(tm // TK):
            rows = slice(b * TK, (b + 1) * TK)
            r = project(h_ref[rows, :])
            p_ref[rows, :] = r.astype(BF16)
            rt = r[:, :SB_WIDTH].T
            for h in range(HEADS):
                vt_ref[h, b] = rt[h * HEAD_DIM:(h + 1) * HEAD_DIM, :].astype(BF16)
            for l in range(slabs):
                s_ref[l, rows, :] = r[:, SB_WIDTH + l * LANES:SB_WIDTH + (l + 1) * LANES]
            out_rows = TK // CHUNK
            u_ref[b * out_rows:(b + 1) * out_rows, :] = jnp.concatenate(
                [s_ref[l, pl.ds(b * TK + c, out_rows, stride=CHUNK), :]
                 for c in range(CHUNK) for l in range(slabs)], axis=1).astype(BF16)

    @pl.when(j >= 2)
    def _():
        p_ref[...] = project(h_ref[...]).astype(BF16)


def _inproj(x, g, w):
    tm, tn = 512, 2048
    assert tn == 2 * SB_WIDTH == SB_WIDTH + SSM_WIDTH and tm % TK == 0 and tm % CHUNK == 0
    return pl.pallas_call(
        _inproj_kernel,
        grid=(SEQ // tm, IN_COLS // tn),
        in_specs=[
            pl.BlockSpec((tm, D_MODEL), lambda i, j: (i, 0)),
            pl.BlockSpec((1, D_MODEL), lambda i, j: (0, 0)),
            pl.BlockSpec((D_MODEL, tn), lambda i, j: (0, j)),
        ],
        out_specs=[
            pl.BlockSpec((tm, tn), lambda i, j: (i, j)),
            pl.BlockSpec((tm // CHUNK, CHUNK * SSM_WIDTH), lambda i, j: (i, 0)),
            pl.BlockSpec((HEADS, tm // TK, HEAD_DIM, TK), lambda i, j: (0, i, 0, 0)),
        ],
        out_shape=[
            jax.ShapeDtypeStruct((SEQ, IN_COLS), BF16),
            jax.ShapeDtypeStruct((N_CHUNKS, CHUNK * SSM_WIDTH), BF16),
            jax.ShapeDtypeStruct((HEADS, SEQ // TK, HEAD_DIM, TK), BF16),
        ],
        scratch_shapes=[pltpu.VMEM((tm, D_MODEL), BF16), pltpu.VMEM((SSM_WIDTH // LANES, tm, LANES), F32)],
        compiler_params=_params("parallel", "arbitrary"),
        name="inproj",
    )(x, g, w)


def _attn_kernel(q_ref, k_ref, vt_ref, o_ref, kmax_ref, carry_ref, acc_ref):
    i = pl.program_id(1)
    hs = lambda h: slice(h * HEAD_DIM, (h + 1) * HEAD_DIM)
    ones = jnp.ones((HEAD_DIM, LANES), BF16)

    def row_norm2(rows_bf16):
        sq = rows_bf16.astype(F32)
        return jnp.dot((sq * sq).astype(BF16), ones, preferred_element_type=F32)

    @pl.when(i == 0)
    def _():
        def body(j, run):
            ks = pl.multiple_of(j * TK, TK)
            new, rows = [], []
            for h in range(HP):
                n2 = jnp.max(row_norm2(k_ref[pl.ds(ks, TK), hs(h)]), axis=0, keepdims=True)[:, :1]
                new.append(jnp.maximum(run[h], n2))
                rows.append(jnp.broadcast_to(jnp.sqrt(new[h]), (1, TQ)))
            kmax_ref[j] = jnp.concatenate(rows, axis=0)
            return tuple(new)

        lax.fori_loop(0, SEQ // TK, body, tuple(jnp.zeros((1, 1), F32) for _ in range(HP)), unroll=4)

    row = lax.broadcasted_iota(jnp.int32, (TK, TQ), 0)
    col = lax.broadcasted_iota(jnp.int32, (TK, TQ), 1)
    upper = jnp.where(col >= row, 1.0, 0.0).astype(BF16)
    causal = row < col

    def block(h, qrows, j, carry, acc, masked):
        ks = pl.multiple_of(j * TK, TK)
        kb = k_ref[pl.ds(ks, TK), hs(h)]
        zt = lax.dot_general(kb, q_ref[qrows, hs(h)], (((1,), (1,)), ((), ())),
                             preferred_element_type=F32)
        sp = jnp.maximum(zt, 0.0) + jnp.log(1.0 + jnp.exp2(-jnp.abs(zt))) * LOG2E
        if masked:
            sp = jnp.where(causal, sp, 0.0)
        rt = jnp.dot(upper, sp.astype(BF16), preferred_element_type=F32)
        w = jnp.exp2(zt - rt - carry)
        if masked:
            w = jnp.where(causal, w, 0.0)
        acc = acc + jnp.dot(vt_ref[h, j], w.astype(BF16), preferred_element_type=F32)
        return carry + rt[0:1, :], acc

    q_blocks = [(qq, i * QB + qq, slice(qq * TQ, (qq + 1) * TQ)) for qq in range(QB)]
    qmax = {}
    for qq, qi, qrows in q_blocks:
        prev = jnp.maximum(qi - 1, 0)
        no_prev = jnp.where(qi == 0, 1e30, 0.0).astype(F32)
        for h in range(HP):
            carry, acc = block(h, qrows, qi, jnp.zeros((1, TQ), F32), jnp.zeros((HEAD_DIM, TQ), F32), True)
            carry, acc = block(h, qrows, prev, carry + no_prev, acc, False)
            carry_ref[qq, h] = carry
            acc_ref[qq, h] = acc
            qf = q_ref[qrows, hs(h)].astype(F32)
            n2 = jnp.max(jnp.sum(qf * qf, axis=1, keepdims=True), axis=0, keepdims=True)
            qmax[qq, h] = jnp.sqrt(n2) * NORM_MARGIN

    def walk_back(qq, qi, qrows):
        def more_needed(j):
            km = kmax_ref[j]
            worst = None
            for h in range(HP):
                slack = carry_ref[qq, h] - (UNDERFLOW_LOG2 + qmax[qq, h] * km[h:h + 1, :])
                worst = slack if worst is None else jnp.minimum(worst, slack)
            return (jnp.min(worst) < 0.0).astype(jnp.int32)

        def cond(c):
            return jnp.logical_and(c[0] >= 0, c[1] > 0)

        def body(c):
            j = c[0]
            for h in range(HP):
                carry, acc = block(h, qrows, j, carry_ref[qq, h], acc_ref[qq, h], False)
                carry_ref[qq, h] = carry
                acc_ref[qq, h] = acc
            return j - 1, more_needed(jnp.maximum(j - 1, 0))

        lax.while_loop(cond, body, (qi - 2, more_needed(jnp.maximum(qi - 2, 0))))

    for qq, qi, qrows in q_blocks:
        walk_back(qq, qi, qrows)
        for h in range(HP):
            o_ref[qrows, hs(h)] = acc_ref[qq, h].T.astype(o_ref.dtype)


def _attention(proj, vt):
    wide = HP * HEAD_DIM
    once = pl.Buffered(1)
    return pl.pallas_call(
        _attn_kernel,
        grid=(HEADS // HP, SEQ // (QB * TQ)),
        in_specs=[
            pl.BlockSpec((QB * TQ, wide), lambda g, i: (i, g)),
            pl.BlockSpec((SEQ, wide), lambda g, i: (0, HEADS // HP + g), pipeline_mode=once),
            pl.BlockSpec((HP, SEQ // TK, HEAD_DIM, TK), lambda g, i: (g, 0, 0, 0), pipeline_mode=once),
        ],
        out_specs=pl.BlockSpec((QB * TQ, wide), lambda g, i: (i, g)),
        out_shape=jax.ShapeDtypeStruct((SEQ, SB_WIDTH), BF16),
        scratch_shapes=[
            pltpu.VMEM((SEQ // TK, HP, TQ), F32),
            pltpu.VMEM((QB, HP, 1, TQ), F32),
            pltpu.VMEM((QB, HP, HEAD_DIM, TQ), F32),
        ],
        compiler_params=_params("arbitrary", "arbitrary"),
        name="attention",
    )(proj, proj, vt)


def _ssm_weights(a_re, a_im, log_dt, b_re, b_im, c_re, c_im, d_skip):
    dt = jnp.exp(log_dt)[:, None]
    d = jnp.arange(CHUNK + 1, dtype=F32)[:, None, None]
    mag = jnp.exp(d * (dt * a_re))
    pw_re = mag * jnp.cos(d * (dt * a_im))
    pw_im = mag * jnp.sin(d * (dt * a_im))
    abar_re, abar_im = pw_re[1], pw_im[1]
    den = a_re * a_re + a_im * a_im
    num_re = abar_re - 1.0
    zoh_re = (num_re * a_re + abar_im * a_im) / den
    zoh_im = (abar_im * a_re - num_re * a_im) / den
    bb_re = zoh_re[..., None] * b_re - zoh_im[..., None] * b_im
    bb_im = zoh_re[..., None] * b_im + zoh_im[..., None] * b_re
    ca_re = c_re[None] * pw_re[:, :, None, :] - c_im[None] * pw_im[:, :, None, :]
    ca_im = c_re[None] * pw_im[:, :, None, :] + c_im[None] * pw_re[:, :, None, :]
    kd = jnp.einsum('dgon,gnp->dgop', ca_re, bb_re) - jnp.einsum('dgon,gnp->dgop', ca_im, bb_im)
    kd = kd.at[0].add(jnp.eye(SSM_GROUP, dtype=F32)[None] * d_skip[:, :, None])
    by_slab = lambda t: t.reshape((N_SLABS, SLAB_GROUPS) + t.shape[1:])
    k_rep = by_slab(kd[:CHUNK].transpose(1, 0, 3, 2)).transpose(0, 2, 3, 1, 4)
    k_rep = k_rep.reshape(N_SLABS, CHUNK, SSM_GROUP, SLAB)
    rev_re, rev_im = pw_re[CHUNK - 1::-1], pw_im[CHUNK - 1::-1]
    wz_re = rev_re[:, :, :, None] * bb_re[None] - rev_im[:, :, :, None] * bb_im[None]
    wz_im = rev_re[:, :, :, None] * bb_im[None] + rev_im[:, :, :, None] * bb_re[None]
    wz = jnp.stack([wz_re, wz_im], axis=0).transpose(2, 1, 4, 0, 3)
    wz_rep = by_slab(wz).transpose(0, 2, 3, 4, 1, 5).reshape(N_SLABS, CHUNK, SSM_GROUP, 2 * SLAB_STATE)
    v = jnp.stack([ca_re[1:], -ca_im[1:]], axis=0).transpose(2, 0, 4, 1, 3)
    v_rep = by_slab(v).transpose(0, 2, 3, 4, 1, 5).reshape(N_SLABS, 2, SSM_STATE, CHUNK * SLAB)
    lam = jnp.stack([pw_re[CHUNK], pw_im[CHUNK]], axis=1)
    lam = by_slab(lam).transpose(0, 2, 1, 3).reshape(N_SLABS, 2, SLAB_STATE)
    wz_mat, t_mat, v_mat = _ssm_expand(k_rep, wz_rep, v_rep)
    return wz_mat, t_mat, v_mat, lam


def _ssm_expand_kernel(k_ref, wz_ref, v_ref, wz_out, t_out, v_out):
    def own_group(rows, log2_per_row, cols, col_period, log2_per_col):
        a = lax.broadcasted_iota(jnp.int32, (rows, cols), 0) >> log2_per_row
        c = (lax.broadcasted_iota(jnp.int32, (rows, cols), 1) & (col_period - 1)) >> log2_per_col
        return a == c

    def expand(vals, mask):
        return jnp.where(mask, jnp.tile(vals, (SLAB_GROUPS, 1)), 0.0).astype(BF16)

    m_t = own_group(SLAB, 4, SLAB, SLAB, 4)
    lag_blocks = [expand(k_ref[d], m_t) for d in range(CHUNK)]
    for j in range(CHUNK):
        for b in range(CHUNK):
            blk = lag_blocks[b - j] if b >= j else jnp.zeros((SLAB, SLAB), BF16)
            t_out[j * SLAB:(j + 1) * SLAB, b * SLAB:(b + 1) * SLAB] = blk
    m_w = own_group(SLAB, 4, 2 * SLAB_STATE, SLAB_STATE, 6)
    for j in range(CHUNK):
        wz_out[j * SLAB:(j + 1) * SLAB, :] = expand(wz_ref[j], m_w)
    m_v = own_group(SLAB_STATE, 6, CHUNK * SLAB, SLAB, 4)
    for r in range(2):
        v_out[r * SLAB_STATE:(r + 1) * SLAB_STATE, :] = expand(v_ref[r], m_v)


def _ssm_expand(k_rep, wz_rep, v_rep):
    assert SSM_GROUP == 1 << 4 and SSM_STATE == 1 << 6
    mat = jax.ShapeDtypeStruct((N_SLABS, CHUNK * SLAB, CHUNK * SLAB), BF16)
    slab3 = lambda shape: pl.BlockSpec((None,) + shape, lambda k: (k, 0, 0))
    return pl.pallas_call(
        _ssm_expand_kernel,
        grid=(N_SLABS,),
        in_specs=[
            pl.BlockSpec((None, CHUNK, SSM_GROUP, SLAB), lambda k: (k, 0, 0, 0)),
            pl.BlockSpec((None, CHUNK, SSM_GROUP, 2 * SLAB_STATE), lambda k: (k, 0, 0, 0)),
            pl.BlockSpec((None, 2, SSM_STATE, CHUNK * SLAB), lambda k: (k, 0, 0, 0)),
        ],
        out_specs=[slab3((CHUNK * SLAB, 2 * SLAB_STATE)), slab3((CHUNK * SLAB, CHUNK * SLAB)),
                   slab3((2 * SLAB_STATE, CHUNK * SLAB))],
        out_shape=[mat, mat, mat],
        compiler_params=_params("parallel"),
        name="ssm_expand",
    )(k_rep, wz_rep, v_rep)


def _cmul(ar, ai, br, bi):
    return ar * br - ai * bi, ar * bi + ai * br


def _ssm_kernel(*refs):
    u_refs = refs[:CHUNK]
    wz_ref, t_ref, v_ref, lam_ref, o_ref, s_ref = refs[CHUNK:]
    ns = SLAB_STATE
    lam1 = (lam_ref[0:1, :], lam_ref[1:2, :])
    lam2 = _cmul(*lam1, *lam1)
    lam4 = _cmul(*lam2, *lam2)
    lam8 = _cmul(*lam4, *lam4)

    def lhs_rows(r0, n):
        return jnp.concatenate([u[pl.ds(r0, n), :] for u in u_refs], axis=1)

    tile_row = lax.broadcasted_iota(jnp.int32, (SUBLANES, ns), 0)
    per_tile = lambda c, shift: jnp.tile(jnp.where(tile_row >= shift, c, 0.0), (SSM_ROWS // SUBLANES, 1))
    levels = [(shift, per_tile(cr, shift), per_tile(ci, shift))
              for shift, (cr, ci) in ((1, lam1), (2, lam2), (4, lam4))]

    def pass1(b, _):
        r0 = pl.multiple_of(b * SSM_ROWS, SSM_ROWS)
        z = jnp.dot(lhs_rows(r0, SSM_ROWS), wz_ref[...], preferred_element_type=F32)
        zr, zi = z[:, :ns], z[:, ns:]
        for shift, cr, ci in levels:
            ar, ai = _cmul(cr, ci, pltpu.roll(zr, shift, axis=0), pltpu.roll(zi, shift, axis=0))
            zr, zi = zr + ar, zi + ai
        s_ref[pl.ds(r0, SSM_ROWS), :ns] = zr
        s_ref[pl.ds(r0, SSM_ROWS), ns:] = zi
        return 0

    lax.fori_loop(0, N_CHUNKS // SSM_ROWS, pass1, 0, unroll=2)

    pow_re, pow_im = [jnp.ones((1, ns), F32)], [jnp.zeros((1, ns), F32)]
    for _ in range(SUBLANES - 1):
        nr, ni = _cmul(pow_re[-1], pow_im[-1], *lam1)
        pow_re.append(nr)
        pow_im.append(ni)
    tab_re = jnp.concatenate(pow_re, axis=0)
    tab_im = jnp.concatenate(pow_im, axis=0)
    sub8 = lax.broadcasted_iota(jnp.int32, (SUBLANES, ns), 0)

    def pass2(t, c):
        cr, ci = c
        r0 = pl.multiple_of(t * SUBLANES, SUBLANES)
        lr = s_ref[pl.ds(r0, SUBLANES), :ns]
        li = s_ref[pl.ds(r0, SUBLANES), ns:]
        er = jnp.where(sub8 >= 1, pltpu.roll(lr, 1, axis=0), 0.0)
        ei = jnp.where(sub8 >= 1, pltpu.roll(li, 1, axis=0), 0.0)
        ar, ai = _cmul(tab_re, tab_im, cr, ci)
        s_ref[pl.ds(r0, SUBLANES), :ns] = er + ar
        s_ref[pl.ds(r0, SUBLANES), ns:] = ei + ai
        nr, ni = _cmul(*lam8, cr, ci)
        return lr[SUBLANES - 1:SUBLANES, :] + nr, li[SUBLANES - 1:SUBLANES, :] + ni

    zero = jnp.zeros((1, ns), F32)
    lax.fori_loop(0, N_CHUNKS // SUBLANES, pass2, (zero, zero), unroll=4)

    for b in range(N_CHUNKS // SSM_ROWS):
        r0 = b * SSM_ROWS
        lhs = lhs_rows(r0, SSM_ROWS)
        state = s_ref[pl.ds(r0, SSM_ROWS), :].astype(BF16)
        for j0 in range(0, CHUNK, 2):
            cols = slice(j0 * SLAB, (j0 + 2) * SLAB)
            k = (j0 + 2) * SLAB
            y = jnp.dot(lhs[:, :k], t_ref[:k, cols], preferred_element_type=F32)
            y = y + jnp.dot(state, v_ref[:, cols], preferred_element_type=F32)
            for j in range(2):
                o_ref[pl.ds(r0 * CHUNK + j0 + j, SSM_ROWS, stride=CHUNK), :] = y[:, j * SLAB:(j + 1) * SLAB]


def _ssm(u2, wz_mat, t_mat, v_mat, lam):
    u_specs = [pl.BlockSpec((N_CHUNKS, SLAB), functools.partial(lambda j, k: (0, j * N_SLABS + k), j))
               for j in range(CHUNK)]
    return pl.pallas_call(
        _ssm_kernel,
        grid=(N_SLABS,),
        in_specs=u_specs + [
            pl.BlockSpec((None, CHUNK * SLAB, 2 * SLAB_STATE), lambda k: (k, 0, 0)),
            pl.BlockSpec((None, CHUNK * SLAB, CHUNK * SLAB), lambda k: (k, 0, 0)),
            pl.BlockSpec((None, 2 * SLAB_STATE, CHUNK * SLAB), lambda k: (k, 0, 0)),
            pl.BlockSpec((None, 2, SLAB_STATE), lambda k: (k, 0, 0)),
        ],
        out_specs=pl.BlockSpec((SEQ, SLAB), lambda k: (0, k), pipeline_mode=pl.Buffered(1)),
        out_shape=jax.ShapeDtypeStruct((SEQ, SSM_WIDTH), F32),
        scratch_shapes=[pltpu.VMEM((N_CHUNKS, 2 * SLAB_STATE), F32)],
        compiler_params=_params("arbitrary"),
        name="ssm",
    )(*([u2] * CHUNK), wz_mat, t_mat, v_mat, lam)


def _gelu_tanh(x):
    c = 0.7978845608028654
    return 0.5 * x * (1.0 + jnp.tanh(c * (x + 0.044715 * (x * x * x))))


def _post_kernel(x_ref, oa_ref, y_ref, ga_ref, gb_ref, wglu_ref, pa_ref, ps_ref, wo_ref, o_ref):
    y = _gelu_tanh(y_ref[...].astype(F32))
    yb = y.astype(BF16)
    glu = jnp.dot(yb, wglu_ref[...], preferred_element_type=F32)
    o_ssm = (y * _sigmoid(glu)).astype(BF16)
    a = jnp.dot(oa_ref[...], pa_ref[...], preferred_element_type=F32)
    b = jnp.dot(o_ssm, ps_ref[...], preferred_element_type=F32)
    merged = _sigmoid(ga_ref[...].astype(F32)) * a + _sigmoid(gb_ref[...].astype(F32)) * b
    o_ref[...] = x_ref[...] + jnp.dot(merged.astype(BF16), wo_ref[...], preferred_element_type=F32)


def _post(x, o_attn, y, proj, w_glu, p_attn, p_ssm, w_out):
    tm = 256
    ga_blk = (3 * SB_WIDTH + SSM_WIDTH) // D_MODEL
    const = lambda shape: pl.BlockSpec(shape, lambda i: (0, 0), pipeline_mode=pl.Buffered(1))
    return pl.pallas_call(
        _post_kernel,
        grid=(SEQ // tm,),
        in_specs=[
            pl.BlockSpec((tm, D_MODEL), lambda i: (i, 0)),
            pl.BlockSpec((tm, SB_WIDTH), lambda i: (i, 0)),
            pl.BlockSpec((tm, SSM_WIDTH), lambda i: (i, 0)),
            pl.BlockSpec((tm, D_MODEL), lambda i: (i, ga_blk)),
            pl.BlockSpec((tm, D_MODEL), lambda i: (i, ga_blk + 1)),
            const((SSM_WIDTH, SSM_WIDTH)),
            const((SB_WIDTH, D_MODEL)),
            const((SSM_WIDTH, D_MODEL)),
            const((D_MODEL, D_MODEL)),
        ],
        out_specs=pl.BlockSpec((tm, D_MODEL), lambda i: (i, 0)),
        out_shape=jax.ShapeDtypeStruct((SEQ, D_MODEL), F32),
        compiler_params=_params("parallel"),
        name="post",
    )(x, o_attn, y, proj, proj, w_glu, p_attn, p_ssm, w_out)


def _ffn_kernel(x_ref, g_ref, wg_ref, wu_ref, wd_ref, o_ref, h_ref):
    f = pl.program_id(1)

    def swiglu(h):
        a = jnp.dot(h, wg_ref[...], preferred_element_type=F32)
        b = jnp.dot(h, wu_ref[...], preferred_element_type=F32)
        t = (a * _sigmoid(a) * b).astype(BF16)
        return jnp.dot(t, wd_ref[...], preferred_element_type=F32)

    @pl.when(f == 0)
    def _():
        piece = x_ref.shape[0] // 4
        for rows in (slice(c * piece, (c + 1) * piece) for c in range(4)):
            x = x_ref[rows, :]
            h = _rms_rows(x, g_ref[...]).astype(BF16)
            h_ref[rows, :] = h
            o_ref[rows, :] = x + swiglu(h)

    @pl.when(f > 0)
    def _():
        o_ref[...] += swiglu(h_ref[...])


def _ffn(x, g, w_gate, w_up, w_down):
    tm, tf = 512, 1024
    return pl.pallas_call(
        _ffn_kernel,
        grid=(SEQ // tm, D_FF_DENSE // tf),
        in_specs=[
            pl.BlockSpec((tm, D_MODEL), lambda i, f: (i, 0)),
            pl.BlockSpec((1, D_MODEL), lambda i, f: (0, 0)),
            pl.BlockSpec((D_MODEL, tf), lambda i, f: (0, f)),
            pl.BlockSpec((D_MODEL, tf), lambda i, f: (0, f)),
            pl.BlockSpec((tf, D_MODEL), lambda i, f: (f, 0)),
        ],
        out_specs=pl.BlockSpec((tm, D_MODEL), lambda i, f: (i, 0)),
        out_shape=jax.ShapeDtypeStruct((SEQ, D_MODEL), F32),
        scratch_shapes=[pltpu.VMEM((tm, D_MODEL), BF16)],
        compiler_params=_params("parallel", "arbitrary"),
        name="ffn",
    )(x, g, w_gate, w_up, w_down)


MOE_TM = 512
MOE_SUB = 2
MOE_ROWS = 160
MOE_PAD = 256


def _moe_kernel(x_ref, g_ref, wr_ref, wg_ref, wu_ref, wd_ref, gf_ref, o_ref,
                h_ref, rw_ref, rk_ref, rkt_ref, cnt_ref, y_ref):
    e = pl.program_id(1)
    tm = MOE_TM
    lane = lax.broadcasted_iota(jnp.int32, (tm, LANES), 1)
    tiles = [slice(u * tm, (u + 1) * tm) for u in range(MOE_SUB)]

    @pl.when(e == 0)
    def _():
        before = (lax.broadcasted_iota(jnp.int32, (tm, tm), 1)
                  < lax.broadcasted_iota(jnp.int32, (tm, tm), 0))
        before = jnp.where(before, 1.0, 0.0).astype(BF16)
        for u, rows in enumerate(tiles):
            x = x_ref[rows, :]
            hf = _rms_rows(x, g_ref[...])
            h_hi = hf.astype(BF16)
            h_ref[rows, :] = h_hi
            o_ref[rows, :] = x
            h_lo = (hf - h_hi.astype(F32)).astype(BF16)
            hi = jnp.dot(h_hi, wr_ref[...], preferred_element_type=F32)
            logits = (hi[:, :LANES] + hi[:, LANES:]
                      + jnp.dot(h_lo, wr_ref[:, :LANES], preferred_element_type=F32))
            lg = jnp.where(lane < N_EXPERTS, logits, -jnp.inf)
            m1 = jnp.max(lg, axis=-1, keepdims=True)
            i1 = jnp.min(jnp.where(lg == m1, lane, LANES), axis=-1, keepdims=True)
            lg2 = jnp.where(lane == i1, -jnp.inf, lg)
            m2 = jnp.max(lg2, axis=-1, keepdims=True)
            i2 = jnp.min(jnp.where(lg2 == m2, lane, LANES), axis=-1, keepdims=True)
            e2 = jnp.exp(m2 - m1)
            g1 = 1.0 / (1.0 + e2)
            g2 = e2 / (1.0 + e2)
            rw_ref[rows, :] = jnp.where(lane == i1, g1, 0.0) + jnp.where(lane == i2, g2, 0.0)
            routed = jnp.logical_or(lane == i1, lane == i2)
            onehot = jnp.where(routed, 1.0, 0.0)
            ahead = jnp.dot(before, onehot.astype(BF16), preferred_element_type=F32)
            rk = jnp.where(routed, ahead, -1.0)
            rk_ref[rows, :] = rk
            rkt_ref[u] = rk.T
            cnt_ref[u] = jnp.sum(onehot, axis=0, keepdims=True)
        y_ref[...] = jnp.zeros_like(y_ref)

    lane1 = lax.broadcasted_iota(jnp.int32, (1, LANES), 1)
    row_id = lax.broadcasted_iota(jnp.int32, (MOE_ROWS, tm), 0).astype(F32)
    col_id = lax.broadcasted_iota(jnp.int32, (tm, MOE_PAD), 1).astype(F32)

    def route_tile(u, rows):
        pick = lambda ref: jnp.sum(jnp.where(lane == e, ref[rows, :], 0.0), axis=-1, keepdims=True)
        rk_col, gate_col = pick(rk_ref), pick(rw_ref)
        rk_row = rkt_ref[u, pl.ds(e, 1), :]
        count = jnp.sum(jnp.where(lane1 == e, cnt_ref[u], 0.0)).astype(jnp.int32)

        def one_pass(s, _):
            base = (s * MOE_ROWS).astype(F32)
            sel = jnp.where(rk_row - base == row_id, 1.0, 0.0).astype(BF16)
            xc = jnp.dot(sel, h_ref[rows, :], preferred_element_type=F32).astype(BF16)
            a = jnp.dot(xc, wg_ref[...], preferred_element_type=F32)
            b = jnp.dot(xc, wu_ref[...], preferred_element_type=F32)
            t = (a * _sigmoid(a) * b).astype(BF16)
            y_ref[:MOE_ROWS, :] = jnp.dot(t, wd_ref[...], preferred_element_type=F32).astype(BF16)
            back = jnp.where(rk_col - base == col_id, 1.0, 0.0).astype(BF16)
            o_ref[rows, :] += gate_col * jnp.dot(back, y_ref[...], preferred_element_type=F32)
            return 0

        lax.fori_loop(0, (count + MOE_ROWS - 1) // MOE_ROWS, one_pass, 0)

    for u, rows in enumerate(tiles):
        route_tile(u, rows)

    @pl.when(e == pl.num_programs(1) - 1)
    def _():
        for rows in tiles:
            o_ref[rows, :] = _rms_rows(o_ref[rows, :], gf_ref[...])


def _moe(x, g, w_router, w_gate, w_up, w_down, g_final):
    tm = MOE_TM * MOE_SUB
    once = pl.Buffered(1)
    return pl.pallas_call(
        _moe_kernel,
        grid=(SEQ // tm, N_EXPERTS),
        in_specs=[
            pl.BlockSpec((tm, D_MODEL), lambda i, e: (i, 0), pipeline_mode=once),
            pl.BlockSpec((1, D_MODEL), lambda i, e: (0, 0)),
            pl.BlockSpec((D_MODEL, 2 * LANES), lambda i, e: (0, 0)),
            pl.BlockSpec((None, D_MODEL, D_FF_EXPERT), lambda i, e: (e, 0, 0)),
            pl.BlockSpec((None, D_MODEL, D_FF_EXPERT), lambda i, e: (e, 0, 0)),
            pl.BlockSpec((None, D_FF_EXPERT, D_MODEL), lambda i, e: (e, 0, 0)),
            pl.BlockSpec((1, D_MODEL), lambda i, e: (0, 0)),
        ],
        out_specs=pl.BlockSpec((tm, D_MODEL), lambda i, e: (i, 0), pipeline_mode=once),
        out_shape=jax.ShapeDtypeStruct((SEQ, D_MODEL), F32),
        scratch_shapes=[
            pltpu.VMEM((tm, D_MODEL), BF16),
            pltpu.VMEM((tm, LANES), F32),
            pltpu.VMEM((tm, LANES), F32),
            pltpu.VMEM((MOE_SUB, LANES, MOE_TM), F32),
            pltpu.VMEM((MOE_SUB, 1, LANES), F32),
            pltpu.VMEM((MOE_PAD, D_MODEL), BF16),
        ],
        compiler_params=_params("parallel", "arbitrary"),
        name="moe",
    )(x, g, w_router, w_gate, w_up, w_down, g_final)


def kernel(x, mix_norm, ffn_norm, final_norm, w_in, ssm_a_re, ssm_a_im, ssm_log_dt, ssm_b_re, ssm_b_im,
           ssm_c_re, ssm_c_im, ssm_d, w_glu, p_attn, p_ssm, w_out, ffn_w_gate, ffn_w_up, ffn_w_down,
           w_router, moe_w_gate, moe_w_up, moe_w_down):
    assert x.shape == (1, SEQ, D_MODEL) and DEPTH == 2
    xs = x.reshape(SEQ, D_MODEL)
    col_scale = jnp.concatenate([jnp.full((SB_WIDTH,), HEAD_DIM ** -0.5 * LOG2E, F32),
                                 jnp.ones((IN_COLS - SB_WIDTH,), F32)])
    for layer in range(DEPTH):
        w_in_b = (w_in[layer] * col_scale).astype(BF16)
        proj, u, vt = _inproj(xs, mix_norm[layer][None], w_in_b)
        o_attn = _attention(proj, vt)
        ssm_w = _ssm_weights(ssm_a_re[layer], ssm_a_im[layer], ssm_log_dt[layer], ssm_b_re[layer],
                             ssm_b_im[layer], ssm_c_re[layer], ssm_c_im[layer], ssm_d[layer])
        y = _ssm(u, *ssm_w)
        xs = _post(xs, o_attn, y, proj, w_glu[layer].astype(BF16), p_attn[layer].astype(BF16),
                   p_ssm[layer].astype(BF16), w_out[layer].astype(BF16))
        i = layer // 2
        if layer % 2 == 0:
            xs = _ffn(xs, ffn_norm[layer][None], ffn_w_gate[i].astype(BF16), ffn_w_up[i].astype(BF16),
                      ffn_w_down[i].astype(BF16))
        else:
            wr = jnp.pad(w_router[i], ((0, 0), (0, LANES - N_EXPERTS)))
            wr_hi = wr.astype(BF16)
            wr = jnp.concatenate([wr_hi, (wr - wr_hi.astype(F32)).astype(BF16)], axis=1)
            xs = _moe(xs, ffn_norm[layer][None], wr, moe_w_gate[i].astype(BF16), moe_w_up[i].astype(BF16),
                      moe_w_down[i].astype(BF16), final_norm[None])
    return xs.reshape(1, SEQ, D_MODEL)
```
